```python
import math
import jax, jax.numpy as jnp
from jax import lax
import numpy as np

D_MODEL = 1024
BATCH = 4
SEQ = 4096
DEPTH = 4

N_MIXERS = 3
D_INNER = D_MODEL
PLE_DIM = 256
CONV_WIDTH = 3
SSM_GROUP = 16
SSM_STATE = 64
SSM_GROUPS = D_INNER // SSM_GROUP
DT_MIN = 1e-3
DT_MAX = 1e-1
RWKV_HEAD = 64
RWKV_HEADS = D_INNER // RWKV_HEAD
DECAY_LORA = 64
ICLR_LORA = 64
RWKV_GN_EPS = 64e-5
LN_EPS = 1e-5
DN_ALPHA = (2 * DEPTH) ** 0.25
DN_BETA = (8 * DEPTH) ** -0.25
N_CONV = (DEPTH + 2) // 3
N_SSM = (DEPTH + 1) // 3
N_RWKV = DEPTH // 3

kernel_name = "hybrid_conv_s5_rwkv7_deepnorm_trunk"


def layer_norm(h, g, b, eps=LN_EPS):
    h32 = h.astype(jnp.float32)
    mu = h32.mean(-1, keepdims=True)
    var = jnp.square(h32 - mu).mean(-1, keepdims=True)
    return ((h32 - mu) * lax.rsqrt(var + eps) * g.astype(jnp.float32) + b.astype(jnp.float32)).astype(h.dtype)


def token_shift(x):
    return jnp.pad(x, ((0, 0), (1, 0), (0, 0)))[:, :-1]


def short_conv_mixer(x, w_in, conv_k, w_out):
    bg, cg, h, z = jnp.split(x @ w_in, 4, axis=-1)
    u = cg * h
    conv = lax.conv_general_dilated(
        u, conv_k[:, None, :].astype(u.dtype), window_strides=(1,),
        padding=[(CONV_WIDTH - 1, 0)], dimension_numbers=("NWC", "WIO", "NWC"),
        feature_group_count=D_INNER)
    return (bg * conv * jax.nn.silu(z)) @ w_out


def _ssm_combine(left, right):
    a_l, b_l = left
    a_r, b_r = right
    return a_r * a_l, a_r * b_l + b_r


def s5_mixer(x, w_in, lam_re, lam_im, log_dt, b_re, b_im, c_re, c_im, d_skip, w_glu, b_glu, w_out):
    f32 = jnp.float32
    u, z = jnp.split(x @ w_in, 2, axis=-1)
    bsz, seqlen, _ = u.shape
    ug = u.astype(f32).reshape(bsz, seqlen, SSM_GROUPS, SSM_GROUP)
    lam = lax.complex(lam_re.astype(f32), lam_im.astype(f32))
    dt = jnp.exp(log_dt.astype(f32))[:, None]
    a_bar = jnp.exp(lam * dt)
    b_mat = lax.complex(b_re.astype(f32), b_im.astype(f32))
    b_bar = ((a_bar - 1.0) / lam)[..., None] * b_mat
    c_mat = lax.complex(c_re.astype(f32), c_im.astype(f32))
    bu = jnp.einsum("blgh,gph->blgp", ug, b_bar)
    a_seq = jnp.broadcast_to(a_bar, bu.shape)
    states = lax.associative_scan(_ssm_combine, (a_seq, bu), axis=1)[1]
    y = jnp.einsum("blgp,ghp->blgh", states, c_mat).real + d_skip.astype(f32).reshape(SSM_GROUPS, SSM_GROUP) * ug
    y = jax.nn.gelu(y.reshape(bsz, seqlen, D_INNER).astype(x.dtype))
    y = y * jax.nn.sigmoid(y @ w_glu + b_glu)
    return (y * jax.nn.silu(z)) @ w_out


def _rwkv_step(state, inp):
    r_t, w_t, k_t, v_t, kk_t, a_t = inp
    sa = jnp.einsum("bhvk,bhk->bhv", state, -kk_t)
    state = (state * w_t[:, :, None, :] + sa[..., None] * (kk_t * a_t)[:, :, None, :]
             + v_t[..., None] * k_t[:, :, None, :])
    out = jnp.einsum("bhvk,bhk->bhv", state, r_t)
    return state, out


def rwkv7_mixer(x, mu, w_rkvz, w0, w1, w2, a0, a1, a2, k_k, k_a, r_k, lnx_g, lnx_b, w_out):
    f32 = jnp.float32
    bsz, seqlen, _ = x.shape
    dx = token_shift(x) - x
    xs = x[None] + dx[None] * mu[:, None, None, :]
    r, k, v, z = jnp.einsum("nbld,nde->nble", xs[:4], w_rkvz)
    xw, xa = xs[4], xs[5]
    w_log = -jax.nn.softplus(-(w0 + jnp.tanh(xw @ w1) @ w2)) - 0.5
    decay = jnp.exp(-jnp.exp(w_log.astype(f32)))
    a = jax.nn.sigmoid(a0 + (xa @ a1) @ a2).astype(f32)
    heads = lambda t: t.astype(f32).reshape(bsz, seqlen, RWKV_HEADS, RWKV_HEAD)
    kk = heads(k * k_k)
    kk = kk / jnp.maximum(jnp.linalg.norm(kk, axis=-1, keepdims=True), 1e-12)
    k = heads(k * (1.0 + (a.astype(k.dtype) - 1.0) * k_a))
    r, v, decay, a = heads(r), heads(v), heads(decay), heads(a)
    seq_first = lambda t: jnp.moveaxis(t, 1, 0)
    s0 = jnp.zeros((bsz, RWKV_HEADS, RWKV_HEAD, RWKV_HEAD), f32)
    _, out = lax.scan(_rwkv_step, s0, tuple(seq_first(t) for t in (r, decay, k, v, kk, a)))
    out = jnp.moveaxis(out, 0, 1)
    mu_o = out.mean(-1, keepdims=True)
    var_o = jnp.square(out - mu_o).mean(-1, keepdims=True)
    out = ((out - mu_o) * lax.rsqrt(var_o + RWKV_GN_EPS)).reshape(bsz, seqlen, D_INNER)
    out = out * lnx_g.astype(f32) + lnx_b.astype(f32)
    bonus = jnp.sum(r * k * r_k.astype(f32), axis=-1, keepdims=True) * v
    out = (out + bonus.reshape(bsz, seqlen, D_INNER)).astype(x.dtype)
    return (out * jax.nn.silu(z)) @ w_out


def setup_inputs(seed: int = 0) -> dict:
    key = jax.random.key(seed)
    ks = iter(jax.random.split(key, 48))
    nrm = lambda shape, s: jax.random.normal(next(ks), shape, jnp.float32) * s
    D, E, G, P, H, N = D_MODEL, D_INNER, SSM_GROUPS, SSM_STATE, RWKV_HEADS, RWKV_HEAD
    ramp = jnp.linspace(0.0, 1.0, E, dtype=jnp.float32) ** 0.85
    inp = {}
    inp["x"] = nrm((BATCH, SEQ, D), 1.0)
    inp["p"] = nrm((DEPTH, BATCH, SEQ, PLE_DIM), 1.0)
    inp["conv_w_in"] = nrm((N_CONV, D, 4 * E), D ** -0.5)
    inp["conv_k"] = nrm((N_CONV, CONV_WIDTH, E), CONV_WIDTH ** -0.5)
    inp["conv_w_out"] = nrm((N_CONV, E, D), E ** -0.5 * DN_BETA)
    inp["ssm_w_in"] = nrm((N_SSM, D, 2 * E), D ** -0.5)
    inp["ssm_lam_re"] = -0.5 + nrm((N_SSM, G, P), 0.01)
    inp["ssm_lam_im"] = math.pi * jnp.arange(P, dtype=jnp.float32) + nrm((N_SSM, G, P), 0.01)
    inp["ssm_log_dt"] = jax.random.uniform(next(ks), (N_SSM, G), jnp.float32, math.log(DT_MIN), math.log(DT_MAX))
    inp["ssm_b_re"] = nrm((N_SSM, G, P, SSM_GROUP), (2 * SSM_GROUP) ** -0.5)
    inp["ssm_b_im"] = nrm((N_SSM, G, P, SSM_GROUP), (2 * SSM_GROUP) ** -0.5)
    inp["ssm_c_re"] = nrm((N_SSM, G, SSM_GROUP, P), (2 * P) ** -0.5)
    inp["ssm_c_im"] = nrm((N_SSM, G, SSM_GROUP, P), (2 * P) ** -0.5)
    inp["ssm_d"] = nrm((N_SSM, E), 1.0)
    inp["ssm_w_glu"] = nrm((N_SSM, E, E), E ** -0.5)
    inp["ssm_b_glu"] = nrm((N_SSM, E), 0.01)
    inp["ssm_w_out"] = nrm((N_SSM, E, D), E ** -0.5 * DN_BETA)
    inp["rwkv_mu"] = jax.random.uniform(next(ks), (N_RWKV, 6, D), jnp.float32)
    inp["rwkv_w_rkvz"] = nrm((N_RWKV, 4, D, E), D ** -0.5)
    inp["rwkv_w0"] = -6.0 + 5.0 * ramp + nrm((N_RWKV, E), 0.1)
    inp["rwkv_w1"] = nrm((N_RWKV, D, DECAY_LORA), D ** -0.5)
    inp["rwkv_w2"] = nrm((N_RWKV, DECAY_LORA, E), 0.1 * DECAY_LORA ** -0.5)
    inp["rwkv_a0"] = nrm((N_RWKV, E), 0.1)
    inp["rwkv_a1"] = nrm((N_RWKV, D, ICLR_LORA), D ** -0.5)
    inp["rwkv_a2"] = nrm((N_RWKV, ICLR_LORA, E), 0.1 * ICLR_LORA ** -0.5)
    inp["rwkv_k_k"] = 0.85 + nrm((N_RWKV, E), 0.05)
    inp["rwkv_k_a"] = 1.0 + nrm((N_RWKV, E), 0.05)
    inp["rwkv_r_k"] = nrm((N_RWKV, H, N), 0.1)
    inp["rwkv_lnx_g"] = 1.0 + nrm((N_RWKV, E), 0.05)
    inp["rwkv_lnx_b"] = nrm((N_RWKV, E), 0.01)
    inp["rwkv_w_out"] = nrm((N_RWKV, E, D), E ** -0.5 * DN_BETA)
    inp["ple_proj"] = nrm((DEPTH, PLE_DIM, D), PLE_DIM ** -0.5)
    inp["ple_gate"] = nrm((DEPTH, D, D), D ** -0.5)
    inp["ln_g"] = 1.0 + nrm((DEPTH, D), 0.05)
    inp["ln_b"] = nrm((DEPTH, D), 0.01)
    return inp


def reference(x, p, conv_w_in, conv_k, conv_w_out,
              ssm_w_in, ssm_lam_re, ssm_lam_im, ssm_log_dt, ssm_b_re, ssm_b_im, ssm_c_re, ssm_c_im,
              ssm_d, ssm_w_glu, ssm_b_glu, ssm_w_out,
              rwkv_mu, rwkv_w_rkvz, rwkv_w0, rwkv_w1, rwkv_w2, rwkv_a0, rwkv_a1, rwkv_a2,
              rwkv_k_k, rwkv_k_a, rwkv_r_k, rwkv_lnx_g, rwkv_lnx_b, rwkv_w_out,
              ple_proj, ple_gate, ln_g, ln_b):
    for i in range(DEPTH):
        kind, j = i % N_MIXERS, i // N_MIXERS
        if kind == 0:
            y = short_conv_mixer(x, conv_w_in[j], conv_k[j], conv_w_out[j])
        elif kind == 1:
            y = s5_mixer(x, ssm_w_in[j], ssm_lam_re[j], ssm_lam_im[j], ssm_log_dt[j],
                         ssm_b_re[j], ssm_b_im[j], ssm_c_re[j], ssm_c_im[j], ssm_d[j],
                         ssm_w_glu[j], ssm_b_glu[j], ssm_w_out[j])
        else:
            y = rwkv7_mixer(x, rwkv_mu[j], rwkv_w_rkvz[j], rwkv_w0[j], rwkv_w1[j], rwkv_w2[j],
                            rwkv_a0[j], rwkv_a1[j], rwkv_a2[j], rwkv_k_k[j], rwkv_k_a[j],
                            rwkv_r_k[j], rwkv_lnx_g[j], rwkv_lnx_b[j], rwkv_w_out[j])
        r = DN_ALPHA * x + y
        r = r + (p[i] @ ple_proj[i]) * jax.nn.sigmoid(r @ ple_gate[i])
        x = layer_norm(r, ln_g[i], ln_b[i])
    return x
```

```python
import functools

import jax
import jax.numpy as jnp
from jax import lax
from jax.experimental import pallas as pl
from jax.experimental.pallas import tpu as pltpu

F32 = jnp.float32
BF16 = jnp.bfloat16

LN_EPS = 1e-5
RWKV_GN_EPS = 64e-5
RWKV_HEAD = 64
SSM_GROUP = 16
SUBLANES = 8
LANES = 128
SLAB_GROUPS = LANES // SSM_GROUP
VMEM_LIMIT = 56 * 1024 * 1024

ROW_TILE = 256
RWKV_CHUNK = 64


def _dot(a, b):
    return jnp.dot(a, b, preferred_element_type=F32)


def _bdot(spec, a, b):
    return jnp.einsum(spec, a.astype(BF16), b.astype(BF16), preferred_element_type=F32)


def _silu(z):
    return z * jax.nn.sigmoid(z)


def _shift_rows(cur, prev, shift):
    rolled = pltpu.roll(cur, shift, 0)
    rows = lax.broadcasted_iota(jnp.int32, (SUBLANES, 1), 0)
    top = jnp.where(rows >= shift, rolled[:SUBLANES], pltpu.roll(prev, shift, 0))
    return jnp.concatenate([top, rolled[SUBLANES:]], axis=0)


def _tail(x, y, p_ref, wpe_ref, wpg_ref, lng_ref, lnb_ref, alpha):
    r = alpha * x + y
    pe = _dot(p_ref[0].astype(BF16), wpe_ref[...])
    r = r + pe * jax.nn.sigmoid(_dot(r.astype(BF16), wpg_ref[...]))
    d = r - jnp.mean(r, axis=-1, keepdims=True)
    var = jnp.mean(d * d, axis=-1, keepdims=True)
    return d * lax.rsqrt(var + LN_EPS) * lng_ref[...] + lnb_ref[...]


def _conv_layer_kernel(x_ref, p_ref, win_ref, ck_ref, wout_ref, wpe_ref, wpg_ref, lng_ref, lnb_ref,
                       o_ref, carry_ref, *, alpha):
    e = wout_ref.shape[0]
    tm = x_ref.shape[1]

    @pl.when(pl.program_id(1) == 0)
    def _():
        carry_ref[...] = jnp.zeros_like(carry_ref)

    x = x_ref[0]
    xb = x.astype(BF16)
    proj = lambda i: _dot(xb, win_ref[:, i * e:(i + 1) * e])
    u = proj(1) * proj(2)
    prev = carry_ref[...]
    carry_ref[...] = u[tm - SUBLANES:, :]
    conv = (ck_ref[0:1, :] * _shift_rows(u, prev, 2) + ck_ref[1:2, :] * _shift_rows(u, prev, 1)
            + ck_ref[2:3, :] * u)
    g = proj(0) * conv * _silu(proj(3))
    y = _dot(g.astype(BF16), wout_ref[...])
    o_ref[0] = _tail(x, y, p_ref, wpe_ref, wpg_ref, lng_ref, lnb_ref, alpha)


def _full(shape):
    return pl.BlockSpec(shape, lambda b, t: (0,) * len(shape))


def _rows(tm, width):
    return pl.BlockSpec((1, tm, width), lambda b, t: (b, t, 0))


def _seq_params():
    return pltpu.CompilerParams(dimension_semantics=("arbitrary", "arbitrary"),
                                vmem_limit_bytes=VMEM_LIMIT)


def _conv_layer(x, p, w_in, conv_k, w_out, wpe, wpg, ln_g, ln_b, alpha):
    b, l, d = x.shape
    e = w_out.shape[0]
    tm = min(ROW_TILE, l)
    return pl.pallas_call(
        functools.partial(_conv_layer_kernel, alpha=alpha),
        grid=(b, l // tm),
        in_specs=[_rows(tm, d), _rows(tm, p.shape[-1]), _full(w_in.shape), _full(conv_k.shape),
                  _full(w_out.shape), _full(wpe.shape), _full(wpg.shape), _full((1, d)), _full((1, d))],
        out_specs=_rows(tm, d),
        out_shape=jax.ShapeDtypeStruct((b, l, d), F32),
        scratch_shapes=[pltpu.VMEM((SUBLANES, e), F32)],
        compiler_params=_seq_params(),
        name="conv_layer",
    )(x, p, w_in.astype(BF16), conv_k, w_out.astype(BF16), wpe.astype(BF16), wpg.astype(BF16),
      ln_g.reshape(1, d), ln_b.reshape(1, d))


def _s5_discretise_kernel(lre_ref, lim_ref, ldt_ref, bre_ref, bim_ref,
                          pre_ref, pim_ref, bbre_ref, bbim_ref):
    lre, lim = lre_ref[...], lim_ref[...]
    dt = jnp.exp(ldt_ref[...])
    mag = jnp.exp(lre * dt)
    are, aim = mag * jnp.cos(lim * dt), mag * jnp.sin(lim * dt)
    den = lre * lre + lim * lim
    cre = ((are - 1.0) * lre + aim * lim) / den
    cim = (aim * lre - (are - 1.0) * lim) / den
    bre, bim = bre_ref[...], bim_ref[...]
    bbre_ref[...] = cre[:, None, :] * bre - cim[:, None, :] * bim
    bbim_ref[...] = cre[:, None, :] * bim + cim[:, None, :] * bre
    pr, pi = are, aim
    for n in range(SUBLANES):
        pre_ref[n] = pr
        pim_ref[n] = pi
        pr, pi = pr * are - pi * aim, pr * aim + pi * are


def _s5_layer_kernel(x_ref, p_ref, win_ref, wbre_ref, wbim_ref, wcre_ref, wcim_ref, tab_ref, dsk_ref,
                     wglu_ref, bglu_ref, wout_ref, wpe_ref, wpg_ref, lng_ref, lnb_ref,
                     o_ref, state_ref, sre_ref, sim_ref, y_ref, *, alpha):
    e = wout_ref.shape[0]
    tm = x_ref.shape[1]
    nslab = wbre_ref.shape[0]
    w = wbre_ref.shape[2]
    ngrp = tm // SUBLANES

    @pl.when(pl.program_id(1) == 0)
    def _():
        state_ref[...] = jnp.zeros_like(state_ref)

    x = x_ref[0]
    xb = x.astype(BF16)
    u = _dot(xb, win_ref[:, :e])
    z = _dot(xb, win_ref[:, e:])
    ub = u.astype(BF16)

    for s in range(nslab):
        us = ub[:, s * LANES:(s + 1) * LANES]
        re = _dot(us, wbre_ref[s]).reshape(ngrp, SUBLANES, w)
        im = _dot(us, wbim_ref[s]).reshape(ngrp, SUBLANES, w)
        for lvl, dist in enumerate((1, 2, 4)):
            pr = tab_ref[s, 2 * lvl][None]
            pi = tab_ref[s, 2 * lvl + 1][None]
            sre, sim = pltpu.roll(re, dist, 1), pltpu.roll(im, dist, 1)
            re, im = re + pr * sre - pi * sim, im + pr * sim + pi * sre
        sre_ref[...] = re.reshape(tm, w)
        sim_ref[...] = im.reshape(tm, w)
        qr, qi = tab_ref[s, 6], tab_ref[s, 7]

        def carry_step(i, c):
            cr, ci = c
            r0 = pl.multiple_of(i * SUBLANES, SUBLANES)
            nr = sre_ref[pl.ds(r0, SUBLANES), :] + qr * cr - qi * ci
            ni = sim_ref[pl.ds(r0, SUBLANES), :] + qr * ci + qi * cr
            sre_ref[pl.ds(r0, SUBLANES), :] = nr
            sim_ref[pl.ds(r0, SUBLANES), :] = ni
            return nr[SUBLANES - 1:, :], ni[SUBLANES - 1:, :]

        cr, ci = lax.fori_loop(0, ngrp, carry_step,
                               (state_ref[2 * s:2 * s + 1, :], state_ref[2 * s + 1:2 * s + 2, :]))
        state_ref[2 * s:2 * s + 1, :] = cr
        state_ref[2 * s + 1:2 * s + 2, :] = ci
        y_ref[:, s * LANES:(s + 1) * LANES] = (_dot(sre_ref[...].astype(BF16), wcre_ref[s])
                                               + _dot(sim_ref[...].astype(BF16), wcim_ref[s]))

    y = jax.nn.gelu(y_ref[...] + dsk_ref[...] * u)
    y = y * jax.nn.sigmoid(_dot(y.astype(BF16), wglu_ref[...]) + bglu_ref[...])
    out = _dot((y * _silu(z)).astype(BF16), wout_ref[...])
    o_ref[0] = _tail(x, out, p_ref, wpe_ref, wpg_ref, lng_ref, lnb_ref, alpha)


def _s5_layer(x, p, w_in, lam_re, lam_im, log_dt, b_re, b_im, c_re, c_im, d_skip, w_glu, b_glu, w_out,
              wpe, wpg, ln_g, ln_b, alpha):
    b, l, d = x.shape
    e = w_out.shape[0]
    g, pst = lam_re.shape
    hch = b_re.shape[-1]
    nslab = g // SLAB_GROUPS
    w = SLAB_GROUPS * pst
    tm = min(ROW_TILE, l)

    gp = jax.ShapeDtypeStruct((SUBLANES, g, pst), F32)
    gb = jax.ShapeDtypeStruct((g, hch, pst), F32)
    pw_re, pw_im, bb_re, bb_im = pl.pallas_call(
        _s5_discretise_kernel, out_shape=(gp, gp, gb, gb), name="s5_discretise",
    )(lam_re, lam_im, log_dt.reshape(g, 1), jnp.swapaxes(b_re, 1, 2), jnp.swapaxes(b_im, 1, 2))

    eye = jnp.eye(SLAB_GROUPS, dtype=F32)
    blk_in = lambda m: jnp.einsum("sghp,gk->sghkp", m.reshape(nslab, SLAB_GROUPS, hch, pst),
                                  eye).reshape(nslab, LANES, w).astype(BF16)
    blk_out = lambda m: jnp.einsum("sghp,gk->skpgh", m.reshape(nslab, SLAB_GROUPS, hch, pst),
                                   eye).reshape(nslab, w, LANES).astype(BF16)
    rows = jnp.arange(SUBLANES)[:, None]
    flat = lambda m: m.reshape(SUBLANES, nslab, w)
    tabs = []
    for dist in (1, 2, 4):
        for m in (pw_re, pw_im):
            tabs.append(jnp.where(rows[:, :, None] >= dist, flat(m)[dist - 1][None], 0.0))
    tabs += [flat(pw_re), flat(pw_im)]
    tab = jnp.transpose(jnp.stack(tabs), (2, 0, 1, 3))

    return pl.pallas_call(
        functools.partial(_s5_layer_kernel, alpha=alpha),
        grid=(b, l // tm),
        in_specs=[_rows(tm, d), _rows(tm, p.shape[-1]), _full(w_in.shape),
                  _full((nslab, LANES, w)), _full((nslab, LANES, w)),
                  _full((nslab, w, LANES)), _full((nslab, w, LANES)),
                  _full((nslab, 8, SUBLANES, w)), _full((1, e)),
                  _full(w_glu.shape), _full((1, e)), _full(w_out.shape), _full(wpe.shape),
                  _full(wpg.shape), _full((1, d)), _full((1, d))],
        out_specs=_rows(tm, d),
        out_shape=jax.ShapeDtypeStruct((b, l, d), F32),
        scratch_shapes=[pltpu.VMEM((2 * nslab, w), F32), pltpu.VMEM((tm, w), F32),
                        pltpu.VMEM((tm, w), F32), pltpu.VMEM((tm, e), F32)],
        compiler_params=_seq_params(),
        name="s5_layer",
    )(x, p, w_in.astype(BF16), blk_in(bb_re), blk_in(bb_im),
      blk_out(c_re), blk_out(-c_im),
      tab, d_skip.reshape(1, e), w_glu.astype(BF16), b_glu.reshape(1, e), w_out.astype(BF16),
      wpe.astype(BF16), wpg.astype(BF16), ln_g.reshape(1, d), ln_b.reshape(1, d))


def _rwkv_proj_kernel(x_ref, mu_ref, wrkvz_ref, w0_ref, w1_ref, w2_ref, a0_ref, a1_ref, a2_ref,
                      r_ref, lw_ref, k_ref, v_ref, a_ref, z_ref, carry_ref):
    tm = x_ref.shape[1]

    @pl.when(pl.program_id(1) == 0)
    def _():
        carry_ref[...] = jnp.zeros_like(carry_ref)

    x = x_ref[0]
    prev = carry_ref[...]
    carry_ref[...] = x[tm - SUBLANES:, :]
    dx = _shift_rows(x, prev, 1) - x
    mix = lambda n: (x + dx * mu_ref[n:n + 1, :]).astype(BF16)
    for n, ref in enumerate((r_ref, k_ref, v_ref, z_ref)):
        ref[0] = _dot(mix(n), wrkvz_ref[n])
    lora_w = _dot(jnp.tanh(_dot(mix(4), w1_ref[...])).astype(BF16), w2_ref[...])
    w_log = -jax.nn.softplus(-(w0_ref[...] + lora_w)) - 0.5
    lw_ref[0] = -jnp.exp(w_log)
    lora_a = _dot(_dot(mix(5), a1_ref[...]).astype(BF16), a2_ref[...])
    a_ref[0] = jax.nn.sigmoid(a0_ref[...] + lora_a)


def _rwkv_chunk_kernel(r_ref, lw_ref, k_ref, v_ref, a_ref, kk_ref, ka_ref, rk_ref, lng_ref, lnb_ref,
                       o_ref, s_ref):
    tc = r_ref.shape[1]
    nh, n = s_ref.shape[0], s_ref.shape[1]

    @pl.when(pl.program_id(1) == 0)
    def _():
        s_ref[...] = jnp.zeros_like(s_ref)

    heads = lambda t: jnp.stack([t[:, n * h:n * (h + 1)] for h in range(nh)])
    row = lax.broadcasted_iota(jnp.int32, (tc, tc), 0)
    col = lax.broadcasted_iota(jnp.int32, (tc, tc), 1)
    incl, strict = (row >= col)[None], (row > col)[None]

    lw = lw_ref[0]
    tri = jnp.where(row >= col, 1.0, 0.0).astype(BF16)
    hi = lw.astype(BF16)
    rem = lw - hi.astype(F32)
    mid = rem.astype(BF16)
    lo = (rem - mid.astype(F32)).astype(BF16)
    g = _dot(tri, hi) + _dot(tri, mid) + _dot(tri, lo)
    g_end = g[tc - 1:, :]

    r, k, v, a = r_ref[0], k_ref[0], v_ref[0], a_ref[0]
    kmod = k * (1.0 + (a - 1.0) * ka_ref[...])
    e_neg = jnp.exp(-g)
    e_end = jnp.exp(g_end - g)
    kkr = heads(k * kk_ref[...])
    kk = kkr / jnp.maximum(jnp.sqrt(jnp.sum(kkr * kkr, axis=-1, keepdims=True)), 1e-12)
    bvec = kk * heads(a)
    v_h = heads(v)
    at = -kk * heads(jnp.exp(g - lw))
    rt = heads(r * jnp.exp(g))
    bt = bvec * heads(e_neg)
    kt = heads(kmod * e_neg)
    bh = bvec * heads(e_end)
    kh = heads(kmod * e_end)
    s0 = s_ref[...]

    aa = _bdot("htk,hsk->hts", jnp.concatenate([at, rt], axis=1), jnp.concatenate([bt, kt], axis=1))
    a_ab = jnp.where(strict, aa[:, :tc, :tc], 0.0)
    a_ak = jnp.where(strict, aa[:, :tc, tc:], 0.0)
    a_rb = jnp.where(incl, aa[:, tc:, :tc], 0.0)
    a_rk = jnp.where(incl, aa[:, tc:, tc:], 0.0)

    rhs = _bdot("htk,hvk->htv", at, s0) + _bdot("hts,hsv->htv", a_ak, v_h)
    eye = jnp.where(row == col, 1.0, 0.0)[None]
    inv, apow = eye + a_ab, a_ab
    span = 2
    while span < tc:
        apow = _bdot("hts,hsu->htu", apow, apow)
        inv = inv + _bdot("hts,hsu->htu", apow, inv)
        span *= 2
    sa = _bdot("hts,hsv->htv", inv, rhs)

    out = _bdot("htk,hvk->htv", rt, s0) + _bdot("hts,hsv->htv", a_rb, sa) + _bdot("hts,hsv->htv", a_rk, v_h)
    s_ref[...] = (s0 * heads(jnp.exp(g_end)) + _bdot("htv,htk->hvk", sa, bh)
                  + _bdot("htv,htk->hvk", v_h, kh))

    mu = jnp.mean(out, axis=-1, keepdims=True)
    dev = out - mu
    var = jnp.mean(dev * dev, axis=-1, keepdims=True)
    bonus = jnp.sum(heads(r * kmod * rk_ref[...]), axis=-1, keepdims=True) * v_h
    merge = lambda t: jnp.concatenate([t[h] for h in range(nh)], axis=-1)
    o_ref[0] = merge(dev * lax.rsqrt(var + RWKV_GN_EPS)) * lng_ref[...] + lnb_ref[...] + merge(bonus)


def _out_layer_kernel(x_ref, p_ref, m_ref, z_ref, wout_ref, wpe_ref, wpg_ref, lng_ref, lnb_ref, o_ref,
                      *, alpha):
    y = _dot((m_ref[0] * _silu(z_ref[0])).astype(BF16), wout_ref[...])
    o_ref[0] = _tail(x_ref[0], y, p_ref, wpe_ref, wpg_ref, lng_ref, lnb_ref, alpha)


def _rwkv_layer(x, p, mu, w_rkvz, w0, w1, w2, a0, a1, a2, k_k, k_a, r_k, lnx_g, lnx_b, w_out,
                wpe, wpg, ln_g, ln_b, alpha):
    b, l, d = x.shape
    e = w_out.shape[0]
    nh = e // RWKV_HEAD
    tm = min(ROW_TILE, l)
    tc = min(RWKV_CHUNK, l)
    act = jax.ShapeDtypeStruct((b, l, e), F32)
    row1 = lambda t: t.reshape(1, e)

    r, lw, k, v, a, z = pl.pallas_call(
        _rwkv_proj_kernel,
        grid=(b, l // tm),
        in_specs=[_rows(tm, d), _full(mu.shape), _full(w_rkvz.shape), _full((1, e)), _full(w1.shape),
                  _full(w2.shape), _full((1, e)), _full(a1.shape), _full(a2.shape)],
        out_specs=[_rows(tm, e)] * 6,
        out_shape=[act] * 6,
        scratch_shapes=[pltpu.VMEM((SUBLANES, d), F32)],
        compiler_params=_seq_params(),
        name="rwkv_proj",
    )(x, mu, w_rkvz.astype(BF16), row1(w0), w1.astype(BF16), w2.astype(BF16), row1(a0),
      a1.astype(BF16), a2.astype(BF16))

    mixed = pl.pallas_call(
        _rwkv_chunk_kernel,
        grid=(b, l // tc),
        in_specs=[_rows(tc, e)] * 5 + [_full((1, e))] * 5,
        out_specs=_rows(tc, e),
        out_shape=act,
        scratch_shapes=[pltpu.VMEM((nh, RWKV_HEAD, RWKV_HEAD), F32)],
        compiler_params=_seq_params(),
        name="rwkv_chunk",
    )(r, lw, k, v, a, row1(k_k), row1(k_a), row1(r_k), row1(lnx_g), row1(lnx_b))

    return pl.pallas_call(
        functools.partial(_out_layer_kernel, alpha=alpha),
        grid=(b, l // tm),
        in_specs=[_rows(tm, d), _rows(tm, p.shape[-1]), _rows(tm, e), _rows(tm, e), _full(w_out.shape),
                  _full(wpe.shape), _full(wpg.shape), _full((1, d)), _full((1, d))],
        out_specs=_rows(tm, d),
        out_shape=jax.ShapeDtypeStruct((b, l, d), F32),
        compiler_params=_seq_params(),
        name="rwkv_out",
    )(x, p, mixed, z, w_out.astype(BF16), wpe.astype(BF16), wpg.astype(BF16),
      ln_g.reshape(1, d), ln_b.reshape(1, d))


def kernel(x, p, conv_w_in, conv_k, conv_w_out, ssm_w_in, ssm_lam_re, ssm_lam_im, ssm_log_dt, ssm_b_re, ssm_b_im, ssm_c_re, ssm_c_im, ssm_d, ssm_w_glu, ssm_b_glu, ssm_w_out, rwkv_mu, rwkv_w_rkvz, rwkv_w0, rwkv_w1, rwkv_w2, rwkv_a0, rwkv_a1, rwkv_a2, rwkv_k_k, rwkv_k_a, rwkv_r_k, rwkv_lnx_g, rwkv_lnx_b, rwkv_w_out, ple_proj, ple_gate, ln_g, ln_b):
    depth = p.shape[0]
    alpha = (2 * depth) ** 0.25
    for i in range(depth):
        kind, j = i % 3, i // 3
        post = (ple_proj[i], ple_gate[i], ln_g[i], ln_b[i], alpha)
        if kind == 0:
            x = _conv_layer(x, p[i], conv_w_in[j], conv_k[j], conv_w_out[j], *post)
        elif kind == 1:
            x = _s5_layer(x, p[i], ssm_w_in[j], ssm_lam_re[j], ssm_lam_im[j], ssm_log_dt[j],
                          ssm_b_re[j], ssm_b_im[j], ssm_c_re[j], ssm_c_im[j], ssm_d[j],
                          ssm_w_glu[j], ssm_b_glu[j], ssm_w_out[j], *post)
        else:
            x = _rwkv_layer(x, p[i], rwkv_mu[j], rwkv_w_rkvz[j], rwkv_w0[j], rwkv_w1[j], rwkv_w2[j],
                            rwkv_a0[j], rwkv_a1[j], rwkv_a2[j], rwkv_k_k[j], rwkv_k_a[j],
                            rwkv_r_k[j].reshape(-1), rwkv_lnx_g[j], rwkv_lnx_b[j], rwkv_w_out[j], *post)
    return x
```

```python
import functools

import jax
import jax.numpy as jnp
from jax import lax
from jax.experimental import pallas as pl
from jax.experimental.pallas import tpu as pltpu

F32 = jnp.float32
BF16 = jnp.bfloat16

LN_EPS = 1e-5
RWKV_GN_EPS = 64e-5
RWKV_HEAD = 64
SSM_GROUP = 16
SUBLANES = 8
LANES = 128
SLAB_GROUPS = LANES // SSM_GROUP
VMEM_LIMIT = 56 * 1024 * 1024

ROW_TILE = 256
RWKV_CHUNK = 64
RWKV_BATCH = 4


def _dot(a, b):
    return jnp.dot(a, b, preferred_element_type=F32)


def _bdot(spec, a, b):
    return jnp.einsum(spec, a.astype(BF16), b.astype(BF16), preferred_element_type=F32)


def _silu(z):
    return z * jax.nn.sigmoid(z)


def _shift_rows(cur, prev, shift):
    rolled = pltpu.roll(cur, shift, 0)
    rows = lax.broadcasted_iota(jnp.int32, (SUBLANES, 1), 0)
    top = jnp.where(rows >= shift, rolled[:SUBLANES], pltpu.roll(prev, shift, 0))
    return jnp.concatenate([top, rolled[SUBLANES:]], axis=0)


def _tail(x, y, p_ref, wpe_ref, wpg_ref, lng_ref, lnb_ref, alpha):
    r = alpha * x + y
    pe = _dot(p_ref[0].astype(BF16), wpe_ref[...])
    r = r + pe * jax.nn.sigmoid(_dot(r.astype(BF16), wpg_ref[...]))
    d = r - jnp.mean(r, axis=-1, keepdims=True)
    var = jnp.mean(d * d, axis=-1, keepdims=True)
    return d * lax.rsqrt(var + LN_EPS) * lng_ref[...] + lnb_ref[...]


def _conv_layer_kernel(x_ref, p_ref, win_ref, ck_ref, wout_ref, wpe_ref, wpg_ref, lng_ref, lnb_ref,
                       o_ref, carry_ref, *, alpha):
    e = wout_ref.shape[0]
    tm = x_ref.shape[1]

    @pl.when(pl.program_id(1) == 0)
    def _():
        carry_ref[...] = jnp.zeros_like(carry_ref)

    x = x_ref[0]
    xb = x.astype(BF16)
    proj = lambda i: _dot(xb, win_ref[:, i * e:(i + 1) * e])
    u = proj(1) * proj(2)
    prev = carry_ref[...]
    carry_ref[...] = u[tm - SUBLANES:, :]
    conv = (ck_ref[0:1, :] * _shift_rows(u, prev, 2) + ck_ref[1:2, :] * _shift_rows(u, prev, 1)
            + ck_ref[2:3, :] * u)
    g = proj(0) * conv * _silu(proj(3))
    y = _dot(g.astype(BF16), wout_ref[...])
    o_ref[0] = _tail(x, y, p_ref, wpe_ref, wpg_ref, lng_ref, lnb_ref, alpha)


def _full(shape):
    return pl.BlockSpec(shape, lambda b, t: (0,) * len(shape))


def _rows(tm, width):
    return pl.BlockSpec((1, tm, width), lambda b, t: (b, t, 0))


def _seq_params():
    return pltpu.CompilerParams(dimension_semantics=("arbitrary", "arbitrary"),
                                vmem_limit_bytes=VMEM_LIMIT)


def _conv_layer(x, p, w_in, conv_k, w_out, wpe, wpg, ln_g, ln_b, alpha):
    b, l, d = x.shape
    e = w_out.shape[0]
    tm = min(ROW_TILE, l)
    return pl.pallas_call(
        functools.partial(_conv_layer_kernel, alpha=alpha),
        grid=(b, l // tm),
        in_specs=[_rows(tm, d), _rows(tm, p.shape[-1]), _full(w_in.shape), _full(conv_k.shape),
                  _full(w_out.shape), _full(wpe.shape), _full(wpg.shape), _full((1, d)), _full((1, d))],
        out_specs=_rows(tm, d),
        out_shape=jax.ShapeDtypeStruct((b, l, d), F32),
        scratch_shapes=[pltpu.VMEM((SUBLANES, e), F32)],
        compiler_params=_seq_params(),
        name="conv_layer",
    )(x, p, w_in.astype(BF16), conv_k, w_out.astype(BF16), wpe.astype(BF16), wpg.astype(BF16),
      ln_g.reshape(1, d), ln_b.reshape(1, d))


def _s5_discretise_kernel(lre_ref, lim_ref, ldt_ref, bre_ref, bim_ref,
                          pre_ref, pim_ref, bbre_ref, bbim_ref):
    lre, lim = lre_ref[...], lim_ref[...]
    dt = jnp.exp(ldt_ref[...])
    mag = jnp.exp(lre * dt)
    are, aim = mag * jnp.cos(lim * dt), mag * jnp.sin(lim * dt)
    den = lre * lre + lim * lim
    cre = ((are - 1.0) * lre + aim * lim) / den
    cim = (aim * lre - (are - 1.0) * lim) / den
    bre, bim = bre_ref[...], bim_ref[...]
    bbre_ref[...] = cre[:, None, :] * bre - cim[:, None, :] * bim
    bbim_ref[...] = cre[:, None, :] * bim + cim[:, None, :] * bre
    pr, pi = are, aim
    for n in range(SUBLANES):
        pre_ref[n] = pr
        pim_ref[n] = pi
        pr, pi = pr * are - pi * aim, pr * aim + pi * are


def _s5_layer_kernel(x_ref, p_ref, win_ref, wbre_ref, wbim_ref, wcre_ref, wcim_ref, tab_ref, dsk_ref,
                     wglu_ref, bglu_ref, wout_ref, wpe_ref, wpg_ref, lng_ref, lnb_ref,
                     o_ref, state_ref, sre_ref, sim_ref, y_ref, *, alpha):
    e = wout_ref.shape[0]
    tm = x_ref.shape[1]
    nslab = wbre_ref.shape[0]
    w = wbre_ref.shape[2]
    ngrp = tm // SUBLANES

    @pl.when(pl.program_id(1) == 0)
    def _():
        state_ref[...] = jnp.zeros_like(state_ref)

    x = x_ref[0]
    xb = x.astype(BF16)
    u = _dot(xb, win_ref[:, :e])
    z = _dot(xb, win_ref[:, e:])
    ub = u.astype(BF16)

    for s in range(nslab):
        us = ub[:, s * LANES:(s + 1) * LANES]
        re = _dot(us, wbre_ref[s]).reshape(ngrp, SUBLANES, w)
        im = _dot(us, wbim_ref[s]).reshape(ngrp, SUBLANES, w)
        for lvl, dist in enumerate((1, 2, 4)):
            pr = tab_ref[s, 2 * lvl][None]
            pi = tab_ref[s, 2 * lvl + 1][None]
            sre, sim = pltpu.roll(re, dist, 1), pltpu.roll(im, dist, 1)
            re, im = re + pr * sre - pi * sim, im + pr * sim + pi * sre
        sre_ref[...] = re.reshape(tm, w)
        sim_ref[...] = im.reshape(tm, w)
        qr, qi = tab_ref[s, 6], tab_ref[s, 7]

        def carry_step(i, c):
            cr, ci = c
            r0 = pl.multiple_of(i * SUBLANES, SUBLANES)
            nr = sre_ref[pl.ds(r0, SUBLANES), :] + qr * cr - qi * ci
            ni = sim_ref[pl.ds(r0, SUBLANES), :] + qr * ci + qi * cr
            sre_ref[pl.ds(r0, SUBLANES), :] = nr
            sim_ref[pl.ds(r0, SUBLANES), :] = ni
            return nr[SUBLANES - 1:, :], ni[SUBLANES - 1:, :]

        cr, ci = lax.fori_loop(0, ngrp, carry_step,
                               (state_ref[2 * s:2 * s + 1, :], state_ref[2 * s + 1:2 * s + 2, :]))
        state_ref[2 * s:2 * s + 1, :] = cr
        state_ref[2 * s + 1:2 * s + 2, :] = ci
        y_ref[:, s * LANES:(s + 1) * LANES] = (_dot(sre_ref[...].astype(BF16), wcre_ref[s])
                                               + _dot(sim_ref[...].astype(BF16), wcim_ref[s]))

    y = jax.nn.gelu(y_ref[...] + dsk_ref[...] * u)
    y = y * jax.nn.sigmoid(_dot(y.astype(BF16), wglu_ref[...]) + bglu_ref[...])
    out = _dot((y * _silu(z)).astype(BF16), wout_ref[...])
    o_ref[0] = _tail(x, out, p_ref, wpe_ref, wpg_ref, lng_ref, lnb_ref, alpha)


def _s5_layer(x, p, w_in, lam_re, lam_im, log_dt, b_re, b_im, c_re, c_im, d_skip, w_glu, b_glu, w_out,
              wpe, wpg, ln_g, ln_b, alpha):
    b, l, d = x.shape
    e = w_out.shape[0]
    g, pst = lam_re.shape
    hch = b_re.shape[-1]
    nslab = g // SLAB_GROUPS
    w = SLAB_GROUPS * pst
    tm = min(ROW_TILE, l)

    gp = jax.ShapeDtypeStruct((SUBLANES, g, pst), F32)
    gb = jax.ShapeDtypeStruct((g, hch, pst), F32)
    pw_re, pw_im, bb_re, bb_im = pl.pallas_call(
        _s5_discretise_kernel, out_shape=(gp, gp, gb, gb), name="s5_discretise",
    )(lam_re, lam_im, log_dt.reshape(g, 1), jnp.swapaxes(b_re, 1, 2), jnp.swapaxes(b_im, 1, 2))

    eye = jnp.eye(SLAB_GROUPS, dtype=F32)
    blk_in = lambda m: jnp.einsum("sghp,gk->sghkp", m.reshape(nslab, SLAB_GROUPS, hch, pst),
                                  eye).reshape(nslab, LANES, w).astype(BF16)
    blk_out = lambda m: jnp.einsum("sghp,gk->skpgh", m.reshape(nslab, SLAB_GROUPS, hch, pst),
                                   eye).reshape(nslab, w, LANES).astype(BF16)
    rows = jnp.arange(SUBLANES)[:, None]
    flat = lambda m: m.reshape(SUBLANES, nslab, w)
    tabs = []
    for dist in (1, 2, 4):
        for m in (pw_re, pw_im):
            tabs.append(jnp.where(rows[:, :, None] >= dist, flat(m)[dist - 1][None], 0.0))
    tabs += [flat(pw_re), flat(pw_im)]
    tab = jnp.transpose(jnp.stack(tabs), (2, 0, 1, 3))

    return pl.pallas_call(
        functools.partial(_s5_layer_kernel, alpha=alpha),
        grid=(b, l // tm),
        in_specs=[_rows(tm, d), _rows(tm, p.shape[-1]), _full(w_in.shape),
                  _full((nslab, LANES, w)), _full((nslab, LANES, w)),
                  _full((nslab, w, LANES)), _full((nslab, w, LANES)),
                  _full((nslab, 8, SUBLANES, w)), _full((1, e)),
                  _full(w_glu.shape), _full((1, e)), _full(w_out.shape), _full(wpe.shape),
                  _full(wpg.shape), _full((1, d)), _full((1, d))],
        out_specs=_rows(tm, d),
        out_shape=jax.ShapeDtypeStruct((b, l, d), F32),
        scratch_shapes=[pltpu.VMEM((2 * nslab, w), F32), pltpu.VMEM((tm, w), F32),
                        pltpu.VMEM((tm, w), F32), pltpu.VMEM((tm, e), F32)],
        compiler_params=_seq_params(),
        name="s5_layer",
    )(x, p, w_in.astype(BF16), blk_in(bb_re), blk_in(bb_im),
      blk_out(c_re), blk_out(-c_im),
      tab, d_skip.reshape(1, e), w_glu.astype(BF16), b_glu.reshape(1, e), w_out.astype(BF16),
      wpe.astype(BF16), wpg.astype(BF16), ln_g.reshape(1, d), ln_b.reshape(1, d))


def _rwkv_proj_kernel(x_ref, mu_ref, wrkvz_ref, w0_ref, w1_ref, w2_ref, a0_ref, a1_ref, a2_ref,
                      r_ref, lw_ref, k_ref, v_ref, a_ref, z_ref, carry_ref):
    tm = x_ref.shape[1]

    @pl.when(pl.program_id(1) == 0)
    def _():
        carry_ref[...] = jnp.zeros_like(carry_ref)

    x = x_ref[0]
    prev = carry_ref[...]
    carry_ref[...] = x[tm - SUBLANES:, :]
    dx = _shift_rows(x, prev, 1) - x
    mix = lambda n: (x + dx * mu_ref[n:n + 1, :]).astype(BF16)
    for n, ref in enumerate((r_ref, k_ref, v_ref, z_ref)):
        ref[0] = _dot(mix(n), wrkvz_ref[n])
    lora_w = _dot(jnp.tanh(_dot(mix(4), w1_ref[...])).astype(BF16), w2_ref[...])
    w_log = -jax.nn.softplus(-(w0_ref[...] + lora_w)) - 0.5
    lw_ref[0] = -jnp.exp(w_log)
    lora_a = _dot(_dot(mix(5), a1_ref[...]).astype(BF16), a2_ref[...])
    a_ref[0] = jax.nn.sigmoid(a0_ref[...] + lora_a)


def _rwkv_chunk_kernel(r_ref, lw_ref, k_ref, v_ref, a_ref, kk_ref, ka_ref, rk_ref, lng_ref, lnb_ref,
                       o_ref, s_ref):
    nb, tc, e = r_ref.shape
    n = s_ref.shape[1]
    nh = s_ref.shape[0] // nb
    rows = nb * tc

    @pl.when(pl.program_id(1) == 0)
    def _():
        s_ref[...] = jnp.zeros_like(s_ref)

    flat = lambda ref: ref[...].reshape(rows, e)
    heads = lambda t: jnp.stack([t[(t.shape[0] // nb) * b:(t.shape[0] // nb) * (b + 1), n * h:n * (h + 1)]
                                 for b in range(nb) for h in range(nh)])
    row = lax.broadcasted_iota(jnp.int32, (tc, tc), 0)
    col = lax.broadcasted_iota(jnp.int32, (tc, tc), 1)
    incl, strict = (row >= col)[None], (row > col)[None]

    lw = flat(lw_ref)
    brow = lax.broadcasted_iota(jnp.int32, (rows, rows), 0)
    bcol = lax.broadcasted_iota(jnp.int32, (rows, rows), 1)
    tri = jnp.where((brow >= bcol) & (brow // tc == bcol // tc), 1.0, 0.0).astype(BF16)
    hi = lw.astype(BF16)
    rem = lw - hi.astype(F32)
    mid = rem.astype(BF16)
    lo = (rem - mid.astype(F32)).astype(BF16)
    g = _dot(tri, hi) + _dot(tri, mid) + _dot(tri, lo)
    g_end = g.reshape(nb, tc, e)[:, tc - 1:, :]

    seg_w = 2 * LANES
    srow = lax.broadcasted_iota(jnp.int32, (seg_w, seg_w), 0)
    scol = lax.broadcasted_iota(jnp.int32, (seg_w, seg_w), 1)
    seg_ones = jnp.where(srow // n == scol // n, 1.0, 0.0).astype(BF16)

    def seg_sum(t):
        t_hi = t.astype(BF16)
        t_lo = (t - t_hi.astype(F32)).astype(BF16)
        return jnp.concatenate(
            [_dot(t_hi[:, j:j + seg_w], seg_ones) + _dot(t_lo[:, j:j + seg_w], seg_ones)
             for j in range(0, e, seg_w)], axis=-1)

    r, k, v, a = flat(r_ref), flat(k_ref), flat(v_ref), flat(a_ref)
    kmod = k * (1.0 + (a - 1.0) * ka_ref[...])
    e_neg = jnp.exp(-g)
    e_end = jnp.exp(g_end - g.reshape(nb, tc, e)).reshape(rows, e)
    kkr = k * kk_ref[...]
    kk = kkr / jnp.maximum(jnp.sqrt(seg_sum(kkr * kkr)), 1e-12)
    bvec = kk * a
    hb = lambda t: heads(t.astype(BF16))
    v_h = hb(v)
    at = hb(-kk * jnp.exp(g - lw))
    rt = hb(r * jnp.exp(g))
    btkt = jnp.concatenate([hb(bvec * e_neg), hb(kmod * e_neg)], axis=1)
    bhkh = jnp.concatenate([hb(bvec * e_end), hb(kmod * e_end)], axis=1)
    s0 = s_ref[...]
    s0b = s0.astype(BF16)
    bdot = lambda spec, x, y: jnp.einsum(spec, x, y, preferred_element_type=F32)

    aa = bdot("htk,hsk->hts", jnp.concatenate([at, rt], axis=1), btkt)
    row2 = lax.broadcasted_iota(jnp.int32, (tc, 2 * tc), 0)
    col2 = lax.broadcasted_iota(jnp.int32, (tc, 2 * tc), 1)
    col2 = jnp.where(col2 >= tc, col2 - tc, col2)
    m_a = jnp.where((row2 > col2)[None], aa[:, :tc, :], 0.0).astype(BF16)
    m_r = jnp.where((row2 >= col2)[None], aa[:, tc:, :], 0.0).astype(BF16)

    rhs = bdot("htk,hvk->htv", at, s0b) + bdot(
        "hts,hsv->htv", m_a, jnp.concatenate([jnp.zeros_like(v_h), v_h], axis=1))
    eye = jnp.where(row == col, 1.0, 0.0)[None]
    apow = m_a[:, :, :tc]
    inv = eye + apow.astype(F32)
    span = 2
    while span < tc:
        apow = bdot("hts,hsu->htu", apow, apow).astype(BF16)
        inv = inv + bdot("hts,hsu->htu", apow, inv.astype(BF16))
        span *= 2
    sa = bdot("hts,hsv->htv", inv.astype(BF16), rhs.astype(BF16))

    sv = jnp.concatenate([sa.astype(BF16), v_h], axis=1)
    out = bdot("htk,hvk->htv", rt, s0b) + bdot("hts,hsv->htv", m_r, sv)
    s_ref[...] = s0 * heads(jnp.exp(g_end).reshape(nb, e)) + bdot("htv,htk->hvk", sv, bhkh)

    out = jnp.concatenate(
        [jnp.concatenate([out[b * nh + h] for h in range(nh)], axis=-1) for b in range(nb)], axis=0)
    dev = out - seg_sum(out) * (1.0 / n)
    var = seg_sum(dev * dev) * (1.0 / n)
    bonus = seg_sum(r * kmod * rk_ref[...]) * v
    res = dev * lax.rsqrt(var + RWKV_GN_EPS) * lng_ref[...] + lnb_ref[...] + bonus
    o_ref[...] = res.reshape(nb, tc, e)


def _out_layer_kernel(x_ref, p_ref, m_ref, z_ref, wout_ref, wpe_ref, wpg_ref, lng_ref, lnb_ref, o_ref,
                      *, alpha):
    y = _dot((m_ref[0] * _silu(z_ref[0])).astype(BF16), wout_ref[...])
    o_ref[0] = _tail(x_ref[0], y, p_ref, wpe_ref, wpg_ref, lng_ref, lnb_ref, alpha)


def _rwkv_layer(x, p, mu, w_rkvz, w0, w1, w2, a0, a1, a2, k_k, k_a, r_k, lnx_g, lnx_b, w_out,
                wpe, wpg, ln_g, ln_b, alpha):
    b, l, d = x.shape
    e = w_out.shape[0]
    nh = e // RWKV_HEAD
    tm = min(ROW_TILE, l)
    tc = min(RWKV_CHUNK, l)
    nb = RWKV_BATCH if b % RWKV_BATCH == 0 else 1
    chunk_rows = pl.BlockSpec((nb, tc, e), lambda i, t: (i, t, 0))
    act = jax.ShapeDtypeStruct((b, l, e), F32)
    row1 = lambda t: t.reshape(1, e)

    r, lw, k, v, a, z = pl.pallas_call(
        _rwkv_proj_kernel,
        grid=(b, l // tm),
        in_specs=[_rows(tm, d), _full(mu.shape), _full(w_rkvz.shape), _full((1, e)), _full(w1.shape),
                  _full(w2.shape), _full((1, e)), _full(a1.shape), _full(a2.shape)],
        out_specs=[_rows(tm, e)] * 6,
        out_shape=[act] * 6,
        scratch_shapes=[pltpu.VMEM((SUBLANES, d), F32)],
        compiler_params=_seq_params(),
        name="rwkv_proj",
    )(x, mu, w_rkvz.astype(BF16), row1(w0), w1.astype(BF16), w2.astype(BF16), row1(a0),
      a1.astype(BF16), a2.astype(BF16))

    mixed = pl.pallas_call(
        _rwkv_chunk_kernel,
        grid=(b // nb, l // tc),
        in_specs=[chunk_rows] * 5 + [_full((1, e))] * 5,
        out_specs=chunk_rows,
        out_shape=act,
        scratch_shapes=[pltpu.VMEM((nb * nh, RWKV_HEAD, RWKV_HEAD), F32)],
        compiler_params=_seq_params(),
        name="rwkv_chunk",
    )(r, lw, k, v, a, row1(k_k), row1(k_a), row1(r_k), row1(lnx_g), row1(lnx_b))

    return pl.pallas_call(
        functools.partial(_out_layer_kernel, alpha=alpha),
        grid=(b, l // tm),
        in_specs=[_rows(tm, d), _rows(tm, p.shape[-1]), _rows(tm, e), _rows(tm, e), _full(w_out.shape),
                  _full(wpe.shape), _full(wpg.shape), _full((1, d)), _full((1, d))],
        out_specs=_rows(tm, d),
        out_shape=jax.ShapeDtypeStruct((b, l, d), F32),
        compiler_params=_seq_params(),
        name="rwkv_out",
    )(x, p, mixed, z, w_out.astype(BF16), wpe.astype(BF16), wpg.astype(BF16),
      ln_g.reshape(1, d), ln_b.reshape(1, d))


def kernel(x, p, conv_w_in, conv_k, conv_w_out, ssm_w_in, ssm_lam_re, ssm_lam_im, ssm_log_dt, ssm_b_re, ssm_b_im, ssm_c_re, ssm_c_im, ssm_d, ssm_w_glu, ssm_b_glu, ssm_w_out, rwkv_mu, rwkv_w_rkvz, rwkv_w0, rwkv_w1, rwkv_w2, rwkv_a0, rwkv_a1, rwkv_a2, rwkv_k_k, rwkv_k_a, rwkv_r_k, rwkv_lnx_g, rwkv_lnx_b, rwkv_w_out, ple_proj, ple_gate, ln_g, ln_b):
    depth = p.shape[0]
    alpha = (2 * depth) ** 0.25
    for i in range(depth):
        kind, j = i % 3, i // 3
        post = (ple_proj[i], ple_gate[i], ln_g[i], ln_b[i], alpha)
        if kind == 0:
            x = _conv_layer(x, p[i], conv_w_in[j], conv_k[j], conv_w_out[j], *post)
        elif kind == 1:
            x = _s5_layer(x, p[i], ssm_w_in[j], ssm_lam_re[j], ssm_lam_im[j], ssm_log_dt[j],
                          ssm_b_re[j], ssm_b_im[j], ssm_c_re[j], ssm_c_im[j], ssm_d[j],
                          ssm_w_glu[j], ssm_b_glu[j], ssm_w_out[j], *post)
        else:
            x = _rwkv_layer(x, p[i], rwkv_mu[j], rwkv_w_rkvz[j], rwkv_w0[j], rwkv_w1[j], rwkv_w2[j],
                            rwkv_a0[j], rwkv_a1[j], rwkv_a2[j], rwkv_k_k[j], rwkv_k_a[j],
                            rwkv_r_k[j].reshape(-1), rwkv_lnx_g[j], rwkv_lnx_b[j], rwkv_w_out[j], *post)
    return x
```

```python
import functools

import jax
import jax.numpy as jnp
from jax import lax
from jax.experimental import pallas as pl
from jax.experimental.pallas import tpu as pltpu

F32 = jnp.float32
BF16 = jnp.bfloat16

LN_EPS = 1e-5
RWKV_GN_EPS = 64e-5
RWKV_HEAD = 64
SSM_GROUP = 16
SUBLANES = 8
LANES = 128
SLAB_GROUPS = LANES // SSM_GROUP
VMEM_LIMIT = 56 * 1024 * 1024

ROW_TILE = 256
RWKV_CHUNK = 64
RWKV_BATCH = 4


def _dot(a, b):
    return jnp.dot(a, b, preferred_element_type=F32)


def _bdot(spec, a, b):
    return jnp.einsum(spec, a.astype(BF16), b.astype(BF16), preferred_element_type=F32)


def _silu(z):
    return z * jax.nn.sigmoid(z)


def _shift_rows(cur, prev, shift):
    rolled = pltpu.roll(cur, shift, 0)
    rows = lax.broadcasted_iota(jnp.int32, (SUBLANES, 1), 0)
    top = jnp.where(rows >= shift, rolled[:SUBLANES], pltpu.roll(prev, shift, 0))
    return jnp.concatenate([top, rolled[SUBLANES:]], axis=0)


def _embed(p_ref, wpe_ref):
    return _dot(p_ref[0, 0].astype(BF16), wpe_ref[...])


def _tail(x, y, pe, wpg_ref, lng_ref, lnb_ref, alpha):
    r = alpha * x + y
    r = r + pe * jax.nn.sigmoid(_dot(r.astype(BF16), wpg_ref[...]))
    d = r - jnp.mean(r, axis=-1, keepdims=True)
    var = jnp.mean(d * d, axis=-1, keepdims=True)
    return d * lax.rsqrt(var + LN_EPS) * lng_ref[...] + lnb_ref[...]


def _conv_layer_kernel(x_ref, p_ref, win_ref, ck_ref, wout_ref, wpe_ref, wpg_ref, lng_ref, lnb_ref,
                       o_ref, carry_ref, *, alpha):
    e = wout_ref.shape[0]
    tm = x_ref.shape[1]

    @pl.when(pl.program_id(1) == 0)
    def _():
        carry_ref[...] = jnp.zeros_like(carry_ref)

    x = x_ref[0]
    xb = x.astype(BF16)
    proj = lambda i: _dot(xb, win_ref[:, i * e:(i + 1) * e])
    u = proj(1) * proj(2)
    prev = carry_ref[...]
    carry_ref[...] = u[tm - SUBLANES:, :]
    conv = (ck_ref[0:1, :] * _shift_rows(u, prev, 2) + ck_ref[1:2, :] * _shift_rows(u, prev, 1)
            + ck_ref[2:3, :] * u)
    g = proj(0) * conv * _silu(proj(3))
    y = _dot(g.astype(BF16), wout_ref[...])
    o_ref[0] = _tail(x, y, _embed(p_ref, wpe_ref), wpg_ref, lng_ref, lnb_ref, alpha)


def _full(shape):
    return pl.BlockSpec(shape, lambda b, t: (0,) * len(shape))


def _rows(tm, width):
    return pl.BlockSpec((1, tm, width), lambda b, t: (b, t, 0))


def _layer_rows(layer, tm, width):
    return pl.BlockSpec((1, 1, tm, width), lambda b, t: (layer, b, t, 0))


def _seq_params():
    return pltpu.CompilerParams(dimension_semantics=("arbitrary", "arbitrary"),
                                vmem_limit_bytes=VMEM_LIMIT)


def _conv_layer(x, p, layer, w_in, conv_k, w_out, wpe, wpg, ln_g, ln_b, alpha):
    b, l, d = x.shape
    e = w_out.shape[0]
    tm = min(ROW_TILE, l)
    return pl.pallas_call(
        functools.partial(_conv_layer_kernel, alpha=alpha),
        grid=(b, l // tm),
        in_specs=[_rows(tm, d), _layer_rows(layer, tm, p.shape[-1]), _full(w_in.shape), _full(conv_k.shape),
                  _full(w_out.shape), _full(wpe.shape), _full(wpg.shape), _full((1, d)), _full((1, d))],
        out_specs=_rows(tm, d),
        out_shape=jax.ShapeDtypeStruct((b, l, d), F32),
        scratch_shapes=[pltpu.VMEM((SUBLANES, e), F32)],
        compiler_params=_seq_params(),
        name="conv_layer",
    )(x, p, w_in.astype(BF16), conv_k, w_out.astype(BF16), wpe.astype(BF16), wpg.astype(BF16),
      ln_g.reshape(1, d), ln_b.reshape(1, d))


def _s5_discretise_kernel(lre_ref, lim_ref, ldt_ref, bre_ref, bim_ref,
                          are_ref, aim_ref, pre_ref, pim_ref, bbre_ref, bbim_ref, *, seg_len):
    lre, lim = lre_ref[...], lim_ref[...]
    dt = jnp.exp(ldt_ref[...])
    mag = jnp.exp(lre * dt)
    are, aim = mag * jnp.cos(lim * dt), mag * jnp.sin(lim * dt)
    den = lre * lre + lim * lim
    cre = ((are - 1.0) * lre + aim * lim) / den
    cim = (aim * lre - (are - 1.0) * lim) / den
    bre, bim = bre_ref[...], bim_ref[...]
    bbre_ref[...] = cre[:, None, :] * bre - cim[:, None, :] * bim
    bbim_ref[...] = cre[:, None, :] * bim + cim[:, None, :] * bre
    are_ref[...] = are
    aim_ref[...] = aim
    sre, sim = are, aim
    for _ in range(seg_len.bit_length() - 1):
        sre, sim = sre * sre - sim * sim, 2.0 * sre * sim
    pr, pi = sre, sim
    for n in range(SUBLANES):
        pre_ref[n] = pr
        pim_ref[n] = pi
        pr, pi = pr * sre - pi * sim, pr * sim + pi * sre


def _s5_layer_kernel(x_ref, p_ref, win_ref, wbre_ref, wbim_ref, wcre_ref, wcim_ref, tab_ref, dsk_ref,
                     wglu_ref, bglu_ref, wout_ref, wpe_ref, wpg_ref, lng_ref, lnb_ref,
                     o_ref, state_ref, sre_ref, sim_ref, y_ref, *, alpha):
    e = wout_ref.shape[0]
    tm = x_ref.shape[1]
    nslab = wbre_ref.shape[0]
    w = wbre_ref.shape[2]
    seg_len = tm // SUBLANES

    @pl.when(pl.program_id(1) == 0)
    def _():
        state_ref[...] = jnp.zeros_like(state_ref)

    x = x_ref[0]
    xb = x.astype(BF16)
    u = _dot(xb, win_ref[:, :e])

    new_r = lax.broadcasted_iota(jnp.int32, (tm, tm), 0)
    old_r = lax.broadcasted_iota(jnp.int32, (tm, tm), 1)
    regroup = jnp.where(old_r == (new_r % SUBLANES) * seg_len + new_r // SUBLANES, 1.0, 0.0).astype(BF16)
    ungroup = jnp.where(new_r == (old_r % SUBLANES) * seg_len + old_r // SUBLANES, 1.0, 0.0).astype(BF16)
    ub = _dot(regroup, u.astype(BF16)).astype(BF16)
    rows = lax.broadcasted_iota(jnp.int32, (SUBLANES, 1), 0)

    def project_in(s):
        us = ub[:, s * LANES:(s + 1) * LANES]
        sre_ref[s] = _dot(us, wbre_ref[s])
        sim_ref[s] = _dot(us, wbim_ref[s])

    z_parts = []
    z_w = 2 * e // nslab
    project_in(0)
    for s in range(nslab):
        if s + 1 < nslab:
            project_in(s + 1)
        if s % 2 == 0:
            z_parts.append(_dot(xb, win_ref[:, e + (s // 2) * z_w:e + (s // 2 + 1) * z_w]))
        if s == 1:
            pe = _embed(p_ref, wpe_ref)
        sre, sim = sre_ref.at[s], sim_ref.at[s]
        ar, ai = tab_ref[s, 0], tab_ref[s, 1]

        def local_final(i, c):
            cr, ci = c
            r0 = pl.multiple_of(i * SUBLANES, SUBLANES)
            return (cr * ar - ci * ai + sre[pl.ds(r0, SUBLANES), :],
                    cr * ai + ci * ar + sim[pl.ds(r0, SUBLANES), :])

        zero = jnp.zeros((SUBLANES, w), F32)
        fr, fi = lax.fori_loop(0, seg_len, local_final, (zero, zero), unroll=True)
        for lvl, dist in enumerate((1, 2, 4)):
            pr, pi = tab_ref[s, 2 + 2 * lvl], tab_ref[s, 3 + 2 * lvl]
            gr, gi = pltpu.roll(fr, dist, 0), pltpu.roll(fi, dist, 0)
            fr, fi = fr + pr * gr - pi * gi, fi + pr * gi + pi * gr
        cr0, ci0 = state_ref[2 * s:2 * s + 1, :], state_ref[2 * s + 1:2 * s + 2, :]
        qr, qi = tab_ref[s, 8], tab_ref[s, 9]
        init_r = qr * cr0 - qi * ci0 + jnp.where(rows >= 1, pltpu.roll(fr, 1, 0), 0.0)
        init_i = qr * ci0 + qi * cr0 + jnp.where(rows >= 1, pltpu.roll(fi, 1, 0), 0.0)

        def all_states(i, c):
            cr, ci = c
            r0 = pl.multiple_of(i * SUBLANES, SUBLANES)
            nr = cr * ar - ci * ai + sre[pl.ds(r0, SUBLANES), :]
            ni = cr * ai + ci * ar + sim[pl.ds(r0, SUBLANES), :]
            sre[pl.ds(r0, SUBLANES), :] = nr
            sim[pl.ds(r0, SUBLANES), :] = ni
            return nr, ni

        lr, li = lax.fori_loop(0, seg_len, all_states, (init_r, init_i), unroll=True)
        state_ref[2 * s:2 * s + 1, :] = lr[SUBLANES - 1:, :]
        state_ref[2 * s + 1:2 * s + 2, :] = li[SUBLANES - 1:, :]
        y_ref[:, s * LANES:(s + 1) * LANES] = (_dot(sre[...].astype(BF16), wcre_ref[s])
                                               + _dot(sim[...].astype(BF16), wcim_ref[s]))

    yg = y_ref[...]
    y_hi = yg.astype(BF16)
    y_rem = yg - y_hi.astype(F32)
    y_mid = y_rem.astype(BF16)
    y_lo = (y_rem - y_mid.astype(F32)).astype(BF16)
    y = _dot(ungroup, y_hi) + _dot(ungroup, y_mid) + _dot(ungroup, y_lo)

    y = jax.nn.gelu(y + dsk_ref[...] * u)
    y = y * jax.nn.sigmoid(_dot(y.astype(BF16), wglu_ref[...]) + bglu_ref[...])
    z = jnp.concatenate(z_parts, axis=-1)
    out = _dot((y * _silu(z)).astype(BF16), wout_ref[...])
    o_ref[0] = _tail(x, out, pe, wpg_ref, lng_ref, lnb_ref, alpha)


def _s5_layer(x, p, layer, w_in, lam_re, lam_im, log_dt, b_re, b_im, c_re, c_im, d_skip, w_glu, b_glu, w_out,
              wpe, wpg, ln_g, ln_b, alpha):
    b, l, d = x.shape
    e = w_out.shape[0]
    g, pst = lam_re.shape
    hch = b_re.shape[-1]
    nslab = g // SLAB_GROUPS
    w = SLAB_GROUPS * pst
    tm = min(ROW_TILE, l)
    seg_len = tm // SUBLANES
    assert seg_len & (seg_len - 1) == 0, "segment length must be a power of two"

    gp1 = jax.ShapeDtypeStruct((g, pst), F32)
    gp = jax.ShapeDtypeStruct((SUBLANES, g, pst), F32)
    gb = jax.ShapeDtypeStruct((g, hch, pst), F32)
    a_re, a_im, pw_re, pw_im, bb_re, bb_im = pl.pallas_call(
        functools.partial(_s5_discretise_kernel, seg_len=seg_len),
        out_shape=(gp1, gp1, gp, gp, gb, gb), name="s5_discretise",
    )(lam_re, lam_im, log_dt.reshape(g, 1), jnp.swapaxes(b_re, 1, 2), jnp.swapaxes(b_im, 1, 2))

    eye = jnp.eye(SLAB_GROUPS, dtype=F32)
    blk_in = lambda m: jnp.einsum("sghp,gk->sghkp", m.reshape(nslab, SLAB_GROUPS, hch, pst),
                                  eye).reshape(nslab, LANES, w).astype(BF16)
    blk_out = lambda m: jnp.einsum("sghp,gk->skpgh", m.reshape(nslab, SLAB_GROUPS, hch, pst),
                                   eye).reshape(nslab, w, LANES).astype(BF16)
    rows = jnp.arange(SUBLANES)[:, None, None]
    flat = lambda m: m.reshape(SUBLANES, nslab, w)
    tabs = [jnp.broadcast_to(m.reshape(1, nslab, w), (SUBLANES, nslab, w)) for m in (a_re, a_im)]
    for dist in (1, 2, 4):
        tabs += [jnp.where(rows >= dist, flat(m)[dist - 1][None], 0.0) for m in (pw_re, pw_im)]
    tabs.append(jnp.concatenate([jnp.ones((1, nslab, w), F32), flat(pw_re)[:SUBLANES - 1]]))
    tabs.append(jnp.concatenate([jnp.zeros((1, nslab, w), F32), flat(pw_im)[:SUBLANES - 1]]))
    tab = jnp.transpose(jnp.stack(tabs), (2, 0, 1, 3))

    return pl.pallas_call(
        functools.partial(_s5_layer_kernel, alpha=alpha),
        grid=(b, l // tm),
        in_specs=[_rows(tm, d), _layer_rows(layer, tm, p.shape[-1]), _full(w_in.shape),
                  _full((nslab, LANES, w)), _full((nslab, LANES, w)),
                  _full((nslab, w, LANES)), _full((nslab, w, LANES)),
                  _full(tab.shape), _full((1, e)),
                  _full(w_glu.shape), _full((1, e)), _full(w_out.shape), _full(wpe.shape),
                  _full(wpg.shape), _full((1, d)), _full((1, d))],
        out_specs=_rows(tm, d),
        out_shape=jax.ShapeDtypeStruct((b, l, d), F32),
        scratch_shapes=[pltpu.VMEM((2 * nslab, w), F32), pltpu.VMEM((nslab, tm, w), F32),
                        pltpu.VMEM((nslab, tm, w), F32), pltpu.VMEM((tm, e), F32)],
        compiler_params=_seq_params(),
        name="s5_layer",
    )(x, p, w_in.astype(BF16), blk_in(bb_re), blk_in(bb_im),
      blk_out(c_re), blk_out(-c_im),
      tab, d_skip.reshape(1, e), w_glu.astype(BF16), b_glu.reshape(1, e), w_out.astype(BF16),
      wpe.astype(BF16), wpg.astype(BF16), ln_g.reshape(1, d), ln_b.reshape(1, d))


def _rwkv_proj_kernel(x_ref, mu_ref, wrkvz_ref, w0_ref, w1_ref, w2_ref, a0_ref, a1_ref, a2_ref,
                      r_ref, lw_ref, k_ref, v_ref, a_ref, z_ref, carry_ref):
    tm = x_ref.shape[1]

    @pl.when(pl.program_id(1) == 0)
    def _():
        carry_ref[...] = jnp.zeros_like(carry_ref)

    x = x_ref[0]
    prev = carry_ref[...]
    carry_ref[...] = x[tm - SUBLANES:, :]
    dx = _shift_rows(x, prev, 1) - x
    mix = lambda n: (x + dx * mu_ref[n:n + 1, :]).astype(BF16)
    for n, ref in enumerate((r_ref, k_ref, v_ref, z_ref)):
        ref[0] = _dot(mix(n), wrkvz_ref[n])
    lora_w = _dot(jnp.tanh(_dot(mix(4), w1_ref[...])).astype(BF16), w2_ref[...])
    w_log = -jax.nn.softplus(-(w0_ref[...] + lora_w)) - 0.5
    lw_ref[0] = -jnp.exp(w_log)
    lora_a = _dot(_dot(mix(5), a1_ref[...]).astype(BF16), a2_ref[...])
    a_ref[0] = jax.nn.sigmoid(a0_ref[...] + lora_a)


def _rwkv_chunk_kernel(r_ref, lw_ref, k_ref, v_ref, a_ref, kk_ref, ka_ref, rk_ref, lng_ref, lnb_ref,
                       o_ref, s_ref):
    nb, tc, e = r_ref.shape
    n = s_ref.shape[1]
    nh = s_ref.shape[0] // nb
    rows = nb * tc

    @pl.when(pl.program_id(1) == 0)
    def _():
        s_ref[...] = jnp.zeros_like(s_ref)

    flat = lambda ref: ref[...].reshape(rows, e)
    heads = lambda t: jnp.stack([t[(t.shape[0] // nb) * b:(t.shape[0] // nb) * (b + 1), n * h:n * (h + 1)]
                                 for b in range(nb) for h in range(nh)])
    row = lax.broadcasted_iota(jnp.int32, (tc, tc), 0)
    col = lax.broadcasted_iota(jnp.int32, (tc, tc), 1)
    incl, strict = (row >= col)[None], (row > col)[None]

    lw = flat(lw_ref)
    brow = lax.broadcasted_iota(jnp.int32, (rows, rows), 0)
    bcol = lax.broadcasted_iota(jnp.int32, (rows, rows), 1)
    tri = jnp.where((brow >= bcol) & (brow // tc == bcol // tc), 1.0, 0.0).astype(BF16)
    hi = lw.astype(BF16)
    rem = lw - hi.astype(F32)
    mid = rem.astype(BF16)
    lo = (rem - mid.astype(F32)).astype(BF16)
    g = _dot(tri, hi) + _dot(tri, mid) + _dot(tri, lo)
    g_end = g.reshape(nb, tc, e)[:, tc - 1:, :]

    seg_w = 2 * LANES
    srow = lax.broadcasted_iota(jnp.int32, (seg_w, seg_w), 0)
    scol = lax.broadcasted_iota(jnp.int32, (seg_w, seg_w), 1)
    seg_ones = jnp.where(srow // n == scol // n, 1.0, 0.0).astype(BF16)

    def seg_sum(t):
        t_hi = t.astype(BF16)
        t_lo = (t - t_hi.astype(F32)).astype(BF16)
        return jnp.concatenate(
            [_dot(t_hi[:, j:j + seg_w], seg_ones) + _dot(t_lo[:, j:j + seg_w], seg_ones)
             for j in range(0, e, seg_w)], axis=-1)

    r, k, v, a = flat(r_ref), flat(k_ref), flat(v_ref), flat(a_ref)
    kmod = k * (1.0 + (a - 1.0) * ka_ref[...])
    e_neg = jnp.exp(-g)
    e_end = jnp.exp(g_end - g.reshape(nb, tc, e)).reshape(rows, e)
    kkr = k * kk_ref[...]
    kk = kkr / jnp.maximum(jnp.sqrt(seg_sum(kkr * kkr)), 1e-12)
    bvec = kk * a
    hb = lambda t: heads(t.astype(BF16))
    v_h = hb(v)
    at = hb(-kk * jnp.exp(g - lw))
    rt = hb(r * jnp.exp(g))
    btkt = jnp.concatenate([hb(bvec * e_neg), hb(kmod * e_neg)], axis=1)
    bhkh = jnp.concatenate([hb(bvec * e_end), hb(kmod * e_end)], axis=1)
    s0 = s_ref[...]
    s0b = s0.astype(BF16)
    bdot = lambda spec, x, y: jnp.einsum(spec, x, y, preferred_element_type=F32)

    aa = bdot("htk,hsk->hts", jnp.concatenate([at, rt], axis=1), btkt)
    row2 = lax.broadcasted_iota(jnp.int32, (tc, 2 * tc), 0)
    col2 = lax.broadcasted_iota(jnp.int32, (tc, 2 * tc), 1)
    col2 = jnp.where(col2 >= tc, col2 - tc, col2)
    m_a = jnp.where((row2 > col2)[None], aa[:, :tc, :], 0.0).astype(BF16)
    m_r = jnp.where((row2 >= col2)[None], aa[:, tc:, :], 0.0).astype(BF16)

    rhs = bdot("htk,hvk->htv", at, s0b) + bdot(
        "hts,hsv->htv", m_a, jnp.concatenate([jnp.zeros_like(v_h), v_h], axis=1))
    eye = jnp.where(row == col, 1.0, 0.0)[None]
    apow = m_a[:, :, :tc]
    inv = eye + apow.astype(F32)
    span = 2
    while span < tc:
        apow = bdot("hts,hsu->htu", apow, apow).astype(BF16)
        inv = inv + bdot("hts,hsu->htu", apow, inv.astype(BF16))
        span *= 2
    sa = bdot("hts,hsv->htv", inv.astype(BF16), rhs.astype(BF16))

    sv = jnp.concatenate([sa.astype(BF16), v_h], axis=1)
    out = bdot("htk,hvk->htv", rt, s0b) + bdot("hts,hsv->htv", m_r, sv)
    s_ref[...] = s0 * heads(jnp.exp(g_end).reshape(nb, e)) + bdot("htv,htk->hvk", sv, bhkh)

    out = jnp.concatenate(
        [jnp.concatenate([out[b * nh + h] for h in range(nh)], axis=-1) for b in range(nb)], axis=0)
    dev = out - seg_sum(out) * (1.0 / n)
    var = seg_sum(dev * dev) * (1.0 / n)
    bonus = seg_sum(r * kmod * rk_ref[...]) * v
    res = dev * lax.rsqrt(var + RWKV_GN_EPS) * lng_ref[...] + lnb_ref[...] + bonus
    o_ref[...] = res.reshape(nb, tc, e)


def _out_layer_kernel(x_ref, p_ref, m_ref, z_ref, wout_ref, wpe_ref, wpg_ref, lng_ref, lnb_ref, o_ref,
                      *, alpha):
    y = _dot((m_ref[0] * _silu(z_ref[0])).astype(BF16), wout_ref[...])
    o_ref[0] = _tail(x_ref[0], y, _embed(p_ref, wpe_ref), wpg_ref, lng_ref, lnb_ref, alpha)


def _rwkv_layer(x, p, layer, mu, w_rkvz, w0, w1, w2, a0, a1, a2, k_k, k_a, r_k, lnx_g, lnx_b, w_out,
                wpe, wpg, ln_g, ln_b, alpha):
    b, l, d = x.shape
    e = w_out.shape[0]
    nh = e // RWKV_HEAD
    tm = min(ROW_TILE, l)
    tc = min(RWKV_CHUNK, l)
    nb = RWKV_BATCH if b % RWKV_BATCH == 0 else 1
    chunk_rows = pl.BlockSpec((nb, tc, e), lambda i, t: (i, t, 0))
    act = jax.ShapeDtypeStruct((b, l, e), F32)
    row1 = lambda t: t.reshape(1, e)

    r, lw, k, v, a, z = pl.pallas_call(
        _rwkv_proj_kernel,
        grid=(b, l // tm),
        in_specs=[_rows(tm, d), _full(mu.shape), _full(w_rkvz.shape), _full((1, e)), _full(w1.shape),
                  _full(w2.shape), _full((1, e)), _full(a1.shape), _full(a2.shape)],
        out_specs=[_rows(tm, e)] * 6,
        out_shape=[act] * 6,
        scratch_shapes=[pltpu.VMEM((SUBLANES, d), F32)],
        compiler_params=_seq_params(),
        name="rwkv_proj",
    )(x, mu, w_rkvz.astype(BF16), row1(w0), w1.astype(BF16), w2.astype(BF16), row1(a0),
      a1.astype(BF16), a2.astype(BF16))

    mixed = pl.pallas_call(
        _rwkv_chunk_kernel,
        grid=(b // nb, l // tc),
        in_specs=[chunk_rows] * 5 + [_full((1, e))] * 5,
        out_specs=chunk_rows,
        out_shape=act,
        scratch_shapes=[pltpu.VMEM((nb * nh, RWKV_HEAD, RWKV_HEAD), F32)],
        compiler_params=_seq_params(),
        name="rwkv_chunk",
    )(r, lw, k, v, a, row1(k_k), row1(k_a), row1(r_k), row1(lnx_g), row1(lnx_b))

    return pl.pallas_call(
        functools.partial(_out_layer_kernel, alpha=alpha),
        grid=(b, l // tm),
        in_specs=[_rows(tm, d), _layer_rows(layer, tm, p.shape[-1]), _rows(tm, e), _rows(tm, e), _full(w_out.shape),
                  _full(wpe.shape), _full(wpg.shape), _full((1, d)), _full((1, d))],
        out_specs=_rows(tm, d),
        out_shape=jax.ShapeDtypeStruct((b, l, d), F32),
        compiler_params=_seq_params(),
        name="rwkv_out",
    )(x, p, mixed, z, w_out.astype(BF16), wpe.astype(BF16), wpg.astype(BF16),
      ln_g.reshape(1, d), ln_b.reshape(1, d))


def kernel(x, p, conv_w_in, conv_k, conv_w_out, ssm_w_in, ssm_lam_re, ssm_lam_im, ssm_log_dt, ssm_b_re, ssm_b_im, ssm_c_re, ssm_c_im, ssm_d, ssm_w_glu, ssm_b_glu, ssm_w_out, rwkv_mu, rwkv_w_rkvz, rwkv_w0, rwkv_w1, rwkv_w2, rwkv_a0, rwkv_a1, rwkv_a2, rwkv_k_k, rwkv_k_a, rwkv_r_k, rwkv_lnx_g, rwkv_lnx_b, rwkv_w_out, ple_proj, ple_gate, ln_g, ln_b):
    depth = p.shape[0]
    alpha = (2 * depth) ** 0.25
    for i in range(depth):
        kind, j = i % 3, i // 3
        post = (ple_proj[i], ple_gate[i], ln_g[i], ln_b[i], alpha)
        if kind == 0:
            x = _conv_layer(x, p, i, conv_w_in[j], conv_k[j], conv_w_out[j], *post)
        elif kind == 1:
            x = _s5_layer(x, p, i, ssm_w_in[j], ssm_lam_re[j], ssm_lam_im[j], ssm_log_dt[j],
                          ssm_b_re[j], ssm_b_im[j], ssm_c_re[j], ssm_c_im[j], ssm_d[j],
                          ssm_w_glu[j], ssm_b_glu[j], ssm_w_out[j], *post)
        else:
            x = _rwkv_layer(x, p, i, rwkv_mu[j], rwkv_w_rkvz[j], rwkv_w0[j], rwkv_w1[j], rwkv_w2[j],
                            rwkv_a0[j], rwkv_a1[j], rwkv_a2[j], rwkv_k_k[j], rwkv_k_a[j],
                            rwkv_r_k[j].reshape(-1), rwkv_lnx_g[j], rwkv_lnx_b[j], rwkv_w_out[j], *post)
    return x
```

```python
import functools

import jax
import jax.numpy as jnp
from jax import lax
from jax.experimental import pallas as pl
from jax.experimental.pallas import tpu as pltpu

F32 = jnp.float32
BF16 = jnp.bfloat16

LN_EPS = 1e-5
RWKV_GN_EPS = 64e-5
RWKV_HEAD = 64
SSM_GROUP = 16
SUBLANES = 8
LANES = 128
SLAB_GROUPS = LANES // SSM_GROUP
VMEM_LIMIT = 56 * 1024 * 1024

ROW_TILE = 256
RWKV_CHUNK = 64
RWKV_BATCH = 4


def _dot(a, b):
    return jnp.dot(a, b, preferred_element_type=F32)


def _silu(z):
    return z * jax.nn.sigmoid(z)


def _shift_rows(cur, prev, shift):
    rolled = pltpu.roll(cur, shift, 0)
    rows = lax.broadcasted_iota(jnp.int32, (SUBLANES, 1), 0)
    top = jnp.where(rows >= shift, rolled[:SUBLANES], pltpu.roll(prev, shift, 0))
    return jnp.concatenate([top, rolled[SUBLANES:]], axis=0)


def _embed(p_ref, wpe_ref):
    return _dot(p_ref[0, 0].astype(BF16), wpe_ref[...])


def _tail(x, y, pe, wpg_ref, lng_ref, lnb_ref, alpha):
    r = alpha * x + y
    r = r + pe * jax.nn.sigmoid(_dot(r.astype(BF16), wpg_ref[...]))
    d = r - jnp.mean(r, axis=-1, keepdims=True)
    var = jnp.mean(d * d, axis=-1, keepdims=True)
    return d * lax.rsqrt(var + LN_EPS) * lng_ref[...] + lnb_ref[...]


def _conv_layer_kernel(x_ref, p_ref, win_ref, ck_ref, wout_ref, wpe_ref, wpg_ref, lng_ref, lnb_ref,
                       o_ref, carry_ref, *, alpha):
    e = wout_ref.shape[0]
    tm = x_ref.shape[1]

    @pl.when(pl.program_id(1) == 0)
    def _():
        carry_ref[...] = jnp.zeros_like(carry_ref)

    x = x_ref[0]
    xb = x.astype(BF16)
    proj = lambda i: _dot(xb, win_ref[:, i * e:(i + 1) * e])
    u = proj(1) * proj(2)
    prev = carry_ref[...]
    carry_ref[...] = u[tm - SUBLANES:, :]
    conv = (ck_ref[0:1, :] * _shift_rows(u, prev, 2) + ck_ref[1:2, :] * _shift_rows(u, prev, 1)
            + ck_ref[2:3, :] * u)
    g = proj(0) * conv * _silu(proj(3))
    y = _dot(g.astype(BF16), wout_ref[...])
    o_ref[0] = _tail(x, y, _embed(p_ref, wpe_ref), wpg_ref, lng_ref, lnb_ref, alpha)


def _full(shape):
    return pl.BlockSpec(shape, lambda b, t: (0,) * len(shape))


def _rows(tm, width):
    return pl.BlockSpec((1, tm, width), lambda b, t: (b, t, 0))


def _layer_rows(layer, tm, width):
    return pl.BlockSpec((1, 1, tm, width), lambda b, t: (layer, b, t, 0))


def _seq_params():
    return pltpu.CompilerParams(dimension_semantics=("arbitrary", "arbitrary"),
                                vmem_limit_bytes=VMEM_LIMIT)


def _conv_layer(x, p, layer, w_in, conv_k, w_out, wpe, wpg, ln_g, ln_b, alpha):
    b, l, d = x.shape
    e = w_out.shape[0]
    tm = min(ROW_TILE, l)
    return pl.pallas_call(
        functools.partial(_conv_layer_kernel, alpha=alpha),
        grid=(b, l // tm),
        in_specs=[_rows(tm, d), _layer_rows(layer, tm, p.shape[-1]), _full(w_in.shape), _full(conv_k.shape),
                  _full(w_out.shape), _full(wpe.shape), _full(wpg.shape), _full((1, d)), _full((1, d))],
        out_specs=_rows(tm, d),
        out_shape=jax.ShapeDtypeStruct((b, l, d), F32),
        scratch_shapes=[pltpu.VMEM((SUBLANES, e), F32)],
        compiler_params=_seq_params(),
        name="conv_layer",
    )(x, p, w_in.astype(BF16), conv_k, w_out.astype(BF16), wpe.astype(BF16), wpg.astype(BF16),
      ln_g.reshape(1, d), ln_b.reshape(1, d))


def _s5_discretise_kernel(lre_ref, lim_ref, ldt_ref, bre_ref, bim_ref,
                          are_ref, aim_ref, pre_ref, pim_ref, bbre_ref, bbim_ref, *, seg_len):
    lre, lim = lre_ref[...], lim_ref[...]
    dt = jnp.exp(ldt_ref[...])
    mag = jnp.exp(lre * dt)
    are, aim = mag * jnp.cos(lim * dt), mag * jnp.sin(lim * dt)
    den = lre * lre + lim * lim
    cre = ((are - 1.0) * lre + aim * lim) / den
    cim = (aim * lre - (are - 1.0) * lim) / den
    bre, bim = bre_ref[...], bim_ref[...]
    bbre_ref[...] = cre[:, None, :] * bre - cim[:, None, :] * bim
    bbim_ref[...] = cre[:, None, :] * bim + cim[:, None, :] * bre
    are_ref[...] = are
    aim_ref[...] = aim
    sre, sim = are, aim
    for _ in range(seg_len.bit_length() - 1):
        sre, sim = sre * sre - sim * sim, 2.0 * sre * sim
    pr, pi = sre, sim
    for n in range(SUBLANES):
        pre_ref[n] = pr
        pim_ref[n] = pi
        pr, pi = pr * sre - pi * sim, pr * sim + pi * sre


def _s5_layer_kernel(x_ref, p_ref, win_ref, wbre_ref, wbim_ref, wcre_ref, wcim_ref, tab_ref, dsk_ref,
                     wglu_ref, bglu_ref, wout_ref, wpe_ref, wpg_ref, lng_ref, lnb_ref,
                     o_ref, state_ref, sre_ref, sim_ref, y_ref, *, alpha):
    e = wout_ref.shape[0]
    tm = x_ref.shape[1]
    nslab = wbre_ref.shape[0]
    w = wbre_ref.shape[2]
    seg_len = tm // SUBLANES

    @pl.when(pl.program_id(1) == 0)
    def _():
        state_ref[...] = jnp.zeros_like(state_ref)

    x = x_ref[0]
    xb = x.astype(BF16)
    u = _dot(xb, win_ref[:, :e])

    new_r = lax.broadcasted_iota(jnp.int32, (tm, tm), 0)
    old_r = lax.broadcasted_iota(jnp.int32, (tm, tm), 1)
    regroup = jnp.where(old_r == (new_r % SUBLANES) * seg_len + new_r // SUBLANES, 1.0, 0.0).astype(BF16)
    ungroup = jnp.where(new_r == (old_r % SUBLANES) * seg_len + old_r // SUBLANES, 1.0, 0.0).astype(BF16)
    ub = _dot(regroup, u.astype(BF16)).astype(BF16)
    rows = lax.broadcasted_iota(jnp.int32, (SUBLANES, 1), 0)

    def project_in(s):
        us = ub[:, s * LANES:(s + 1) * LANES]
        sre_ref[s] = _dot(us, wbre_ref[s])
        sim_ref[s] = _dot(us, wbim_ref[s])

    z_parts = []
    z_w = 2 * e // nslab
    project_in(0)
    for s in range(nslab):
        if s + 1 < nslab:
            project_in(s + 1)
        if s % 2 == 0:
            z_parts.append(_dot(xb, win_ref[:, e + (s // 2) * z_w:e + (s // 2 + 1) * z_w]))
        if s == 1:
            pe = _embed(p_ref, wpe_ref)
        sre, sim = sre_ref.at[s], sim_ref.at[s]
        ar, ai = tab_ref[s, 0], tab_ref[s, 1]

        def local_final(i, c):
            cr, ci = c
            r0 = pl.multiple_of(i * SUBLANES, SUBLANES)
            return (cr * ar - ci * ai + sre[pl.ds(r0, SUBLANES), :],
                    cr * ai + ci * ar + sim[pl.ds(r0, SUBLANES), :])

        zero = jnp.zeros((SUBLANES, w), F32)
        fr, fi = lax.fori_loop(0, seg_len, local_final, (zero, zero), unroll=True)
        for lvl, dist in enumerate((1, 2, 4)):
            pr, pi = tab_ref[s, 2 + 2 * lvl], tab_ref[s, 3 + 2 * lvl]
            gr, gi = pltpu.roll(fr, dist, 0), pltpu.roll(fi, dist, 0)
            fr, fi = fr + pr * gr - pi * gi, fi + pr * gi + pi * gr
        cr0, ci0 = state_ref[2 * s:2 * s + 1, :], state_ref[2 * s + 1:2 * s + 2, :]
        qr, qi = tab_ref[s, 8], tab_ref[s, 9]
        init_r = qr * cr0 - qi * ci0 + jnp.where(rows >= 1, pltpu.roll(fr, 1, 0), 0.0)
        init_i = qr * ci0 + qi * cr0 + jnp.where(rows >= 1, pltpu.roll(fi, 1, 0), 0.0)

        def all_states(i, c):
            cr, ci = c
            r0 = pl.multiple_of(i * SUBLANES, SUBLANES)
            nr = cr * ar - ci * ai + sre[pl.ds(r0, SUBLANES), :]
            ni = cr * ai + ci * ar + sim[pl.ds(r0, SUBLANES), :]
            sre[pl.ds(r0, SUBLANES), :] = nr
            sim[pl.ds(r0, SUBLANES), :] = ni
            return nr, ni

        lr, li = lax.fori_loop(0, seg_len, all_states, (init_r, init_i), unroll=True)
        state_ref[2 * s:2 * s + 1, :] = lr[SUBLANES - 1:, :]
        state_ref[2 * s + 1:2 * s + 2, :] = li[SUBLANES - 1:, :]
        y_ref[:, s * LANES:(s + 1) * LANES] = (_dot(sre[...].astype(BF16), wcre_ref[s])
                                               + _dot(sim[...].astype(BF16), wcim_ref[s]))

    yg = y_ref[...]
    y_hi = yg.astype(BF16)
    y_rem = yg - y_hi.astype(F32)
    y_mid = y_rem.astype(BF16)
    y_lo = (y_rem - y_mid.astype(F32)).astype(BF16)
    y = _dot(ungroup, y_hi) + _dot(ungroup, y_mid) + _dot(ungroup, y_lo)

    y = jax.nn.gelu(y + dsk_ref[...] * u)
    y = y * jax.nn.sigmoid(_dot(y.astype(BF16), wglu_ref[...]) + bglu_ref[...])
    z = jnp.concatenate(z_parts, axis=-1)
    out = _dot((y * _silu(z)).astype(BF16), wout_ref[...])
    o_ref[0] = _tail(x, out, pe, wpg_ref, lng_ref, lnb_ref, alpha)


def _s5_layer(x, p, layer, w_in, lam_re, lam_im, log_dt, b_re, b_im, c_re, c_im, d_skip, w_glu, b_glu, w_out,
              wpe, wpg, ln_g, ln_b, alpha):
    b, l, d = x.shape
    e = w_out.shape[0]
    g, pst = lam_re.shape
    hch = b_re.shape[-1]
    nslab = g // SLAB_GROUPS
    w = SLAB_GROUPS * pst
    tm = min(ROW_TILE, l)
    seg_len = tm // SUBLANES
    assert seg_len & (seg_len - 1) == 0, "segment length must be a power of two"

    gp1 = jax.ShapeDtypeStruct((g, pst), F32)
    gp = jax.ShapeDtypeStruct((SUBLANES, g, pst), F32)
    gb = jax.ShapeDtypeStruct((g, hch, pst), F32)
    a_re, a_im, pw_re, pw_im, bb_re, bb_im = pl.pallas_call(
        functools.partial(_s5_discretise_kernel, seg_len=seg_len),
        out_shape=(gp1, gp1, gp, gp, gb, gb), name="s5_discretise",
    )(lam_re, lam_im, log_dt.reshape(g, 1), jnp.swapaxes(b_re, 1, 2), jnp.swapaxes(b_im, 1, 2))

    eye = jnp.eye(SLAB_GROUPS, dtype=F32)
    blk_in = lambda m: jnp.einsum("sghp,gk->sghkp", m.reshape(nslab, SLAB_GROUPS, hch, pst),
                                  eye).reshape(nslab, LANES, w).astype(BF16)
    blk_out = lambda m: jnp.einsum("sghp,gk->skpgh", m.reshape(nslab, SLAB_GROUPS, hch, pst),
                                   eye).reshape(nslab, w, LANES).astype(BF16)
    rows = jnp.arange(SUBLANES)[:, None, None]
    flat = lambda m: m.reshape(SUBLANES, nslab, w)
    tabs = [jnp.broadcast_to(m.reshape(1, nslab, w), (SUBLANES, nslab, w)) for m in (a_re, a_im)]
    for dist in (1, 2, 4):
        tabs += [jnp.where(rows >= dist, flat(m)[dist - 1][None], 0.0) for m in (pw_re, pw_im)]
    tabs.append(jnp.concatenate([jnp.ones((1, nslab, w), F32), flat(pw_re)[:SUBLANES - 1]]))
    tabs.append(jnp.concatenate([jnp.zeros((1, nslab, w), F32), flat(pw_im)[:SUBLANES - 1]]))
    tab = jnp.transpose(jnp.stack(tabs), (2, 0, 1, 3))

    return pl.pallas_call(
        functools.partial(_s5_layer_kernel, alpha=alpha),
        grid=(b, l // tm),
        in_specs=[_rows(tm, d), _layer_rows(layer, tm, p.shape[-1]), _full(w_in.shape),
                  _full((nslab, LANES, w)), _full((nslab, LANES, w)),
                  _full((nslab, w, LANES)), _full((nslab, w, LANES)),
                  _full(tab.shape), _full((1, e)),
                  _full(w_glu.shape), _full((1, e)), _full(w_out.shape), _full(wpe.shape),
                  _full(wpg.shape), _full((1, d)), _full((1, d))],
        out_specs=_rows(tm, d),
        out_shape=jax.ShapeDtypeStruct((b, l, d), F32),
        scratch_shapes=[pltpu.VMEM((2 * nslab, w), F32), pltpu.VMEM((nslab, tm, w), F32),
                        pltpu.VMEM((nslab, tm, w), F32), pltpu.VMEM((tm, e), F32)],
        compiler_params=_seq_params(),
        name="s5_layer",
    )(x, p, w_in.astype(BF16), blk_in(bb_re), blk_in(bb_im),
      blk_out(c_re), blk_out(-c_im),
      tab, d_skip.reshape(1, e), w_glu.astype(BF16), b_glu.reshape(1, e), w_out.astype(BF16),
      wpe.astype(BF16), wpg.astype(BF16), ln_g.reshape(1, d), ln_b.reshape(1, d))


def _rwkv_layer_kernel(x_ref, p_ref, mu_ref, wrkvz_ref, w0_ref, w1_ref, w2_ref, a0_ref, a1_ref, a2_ref,
                       kk_ref, ka_ref, rk_ref, lgx_ref, lbx_ref, wout_ref, wpe_ref, wpg_ref, lng_ref, lnb_ref,
                       o_ref, s_ref, carry_ref, *, alpha):
    nb, tc, d = x_ref.shape
    e = wout_ref.shape[0]
    n = s_ref.shape[1]
    nh = s_ref.shape[0] // nb
    rows = nb * tc

    @pl.when(pl.program_id(1) == 0)
    def _():
        s_ref[...] = jnp.zeros_like(s_ref)
        carry_ref[...] = jnp.zeros_like(carry_ref)

    x3 = x_ref[...]
    x = x3.reshape(rows, d)
    shifted = jnp.concatenate([_shift_rows(x3[b], carry_ref[b], 1) for b in range(nb)], axis=0)
    carry_ref[...] = x3[:, tc - SUBLANES:, :]
    dx = shifted - x
    mix = lambda i: (x + dx * mu_ref[i:i + 1, :]).astype(BF16)
    r, k, v, z = (_dot(mix(i), wrkvz_ref[i]) for i in range(4))
    lora_w = _dot(jnp.tanh(_dot(mix(4), w1_ref[...])).astype(BF16), w2_ref[...])
    w_log = -jax.nn.softplus(-(w0_ref[...] + lora_w)) - 0.5
    lw = -jnp.exp(w_log)
    lora_a = _dot(_dot(mix(5), a1_ref[...]).astype(BF16), a2_ref[...])
    a = jax.nn.sigmoid(a0_ref[...] + lora_a)

    heads = lambda t: jnp.stack([t[(t.shape[0] // nb) * b:(t.shape[0] // nb) * (b + 1), n * h:n * (h + 1)]
                                 for b in range(nb) for h in range(nh)])
    row = lax.broadcasted_iota(jnp.int32, (tc, tc), 0)
    col = lax.broadcasted_iota(jnp.int32, (tc, tc), 1)

    brow = lax.broadcasted_iota(jnp.int32, (rows, rows), 0)
    bcol = lax.broadcasted_iota(jnp.int32, (rows, rows), 1)
    tri = jnp.where((brow >= bcol) & (brow // tc == bcol // tc), 1.0, 0.0).astype(BF16)
    hi = lw.astype(BF16)
    rem = lw - hi.astype(F32)
    mid = rem.astype(BF16)
    lo = (rem - mid.astype(F32)).astype(BF16)
    g = _dot(tri, hi) + _dot(tri, mid) + _dot(tri, lo)
    g_end = g.reshape(nb, tc, e)[:, tc - 1:, :]

    seg_w = 2 * LANES
    srow = lax.broadcasted_iota(jnp.int32, (seg_w, seg_w), 0)
    scol = lax.broadcasted_iota(jnp.int32, (seg_w, seg_w), 1)
    seg_ones = jnp.where(srow // n == scol // n, 1.0, 0.0).astype(BF16)

    def seg_sum(t):
        t_hi = t.astype(BF16)
        t_lo = (t - t_hi.astype(F32)).astype(BF16)
        return jnp.concatenate(
            [_dot(t_hi[:, j:j + seg_w], seg_ones) + _dot(t_lo[:, j:j + seg_w], seg_ones)
             for j in range(0, e, seg_w)], axis=-1)

    kmod = k * (1.0 + (a - 1.0) * ka_ref[...])
    e_neg = jnp.exp(-g)
    e_end = jnp.exp(g_end - g.reshape(nb, tc, e)).reshape(rows, e)
    kkr = k * kk_ref[...]
    kk = kkr / jnp.maximum(jnp.sqrt(seg_sum(kkr * kkr)), 1e-12)
    bvec = kk * a
    hb = lambda t: heads(t.astype(BF16))
    v_h = hb(v)
    at = hb(-kk * jnp.exp(g - lw))
    rt = hb(r * jnp.exp(g))
    btkt = jnp.concatenate([hb(bvec * e_neg), hb(kmod * e_neg)], axis=1)
    bhkh = jnp.concatenate([hb(bvec * e_end), hb(kmod * e_end)], axis=1)
    s0 = s_ref[...]
    s0b = s0.astype(BF16)
    bdot = lambda spec, x, y: jnp.einsum(spec, x, y, preferred_element_type=F32)

    aa = bdot("htk,hsk->hts", jnp.concatenate([at, rt], axis=1), btkt)
    row2 = lax.broadcasted_iota(jnp.int32, (tc, 2 * tc), 0)
    col2 = lax.broadcasted_iota(jnp.int32, (tc, 2 * tc), 1)
    col2 = jnp.where(col2 >= tc, col2 - tc, col2)
    m_a = jnp.where((row2 > col2)[None], aa[:, :tc, :], 0.0).astype(BF16)
    m_r = jnp.where((row2 >= col2)[None], aa[:, tc:, :], 0.0).astype(BF16)

    rhs = bdot("htk,hvk->htv", at, s0b) + bdot(
        "hts,hsv->htv", m_a, jnp.concatenate([jnp.zeros_like(v_h), v_h], axis=1))
    eye = jnp.where(row == col, 1.0, 0.0)[None]
    apow = m_a[:, :, :tc]
    inv = eye + apow.astype(F32)
    span = 2
    while span < tc:
        apow = bdot("hts,hsu->htu", apow, apow).astype(BF16)
        inv = inv + bdot("hts,hsu->htu", apow, inv.astype(BF16))
        span *= 2
    sa = bdot("hts,hsv->htv", inv.astype(BF16), rhs.astype(BF16))

    sv = jnp.concatenate([sa.astype(BF16), v_h], axis=1)
    out = bdot("htk,hvk->htv", rt, s0b) + bdot("hts,hsv->htv", m_r, sv)
    s_ref[...] = s0 * heads(jnp.exp(g_end).reshape(nb, e)) + bdot("htv,htk->hvk", sv, bhkh)

    out = jnp.concatenate(
        [jnp.concatenate([out[b * nh + h] for h in range(nh)], axis=-1) for b in range(nb)], axis=0)
    dev = out - seg_sum(out) * (1.0 / n)
    var = seg_sum(dev * dev) * (1.0 / n)
    bonus = seg_sum(r * kmod * rk_ref[...]) * v
    res = dev * lax.rsqrt(var + RWKV_GN_EPS) * lgx_ref[...] + lbx_ref[...] + bonus

    y = _dot((res * _silu(z)).astype(BF16), wout_ref[...])
    pe = _dot(p_ref[0].reshape(rows, p_ref.shape[-1]).astype(BF16), wpe_ref[...])
    o_ref[...] = _tail(x, y, pe, wpg_ref, lng_ref, lnb_ref, alpha).reshape(nb, tc, d)


def _rwkv_layer(x, p, layer, mu, w_rkvz, w0, w1, w2, a0, a1, a2, k_k, k_a, r_k, lnx_g, lnx_b, w_out,
                wpe, wpg, ln_g, ln_b, alpha):
    b, l, d = x.shape
    e = w_out.shape[0]
    nh = e // RWKV_HEAD
    tc = min(RWKV_CHUNK, l)
    nb = RWKV_BATCH if b % RWKV_BATCH == 0 else 1
    row1 = lambda t: t.reshape(1, -1)
    chunk_rows = lambda width: pl.BlockSpec((nb, tc, width), lambda i, t: (i, t, 0))
    return pl.pallas_call(
        functools.partial(_rwkv_layer_kernel, alpha=alpha),
        grid=(b // nb, l // tc),
        in_specs=[chunk_rows(d), pl.BlockSpec((1, nb, tc, p.shape[-1]), lambda i, t: (layer, i, t, 0)),
                  _full(mu.shape), _full(w_rkvz.shape), _full((1, e)), _full(w1.shape), _full(w2.shape),
                  _full((1, e)), _full(a1.shape), _full(a2.shape)] + [_full((1, e))] * 5
                 + [_full(w_out.shape), _full(wpe.shape), _full(wpg.shape), _full((1, d)), _full((1, d))],
        out_specs=chunk_rows(d),
        out_shape=jax.ShapeDtypeStruct((b, l, d), F32),
        scratch_shapes=[pltpu.VMEM((nb * nh, RWKV_HEAD, RWKV_HEAD), F32),
                        pltpu.VMEM((nb, SUBLANES, d), F32)],
        compiler_params=_seq_params(),
        name="rwkv_layer",
    )(x, p, mu, w_rkvz.astype(BF16), row1(w0), w1.astype(BF16), w2.astype(BF16), row1(a0),
      a1.astype(BF16), a2.astype(BF16), row1(k_k), row1(k_a), row1(r_k), row1(lnx_g), row1(lnx_b),
      w_out.astype(BF16), wpe.astype(BF16), wpg.astype(BF16), row1(ln_g), row1(ln_b))


def kernel(x, p, conv_w_in, conv_k, conv_w_out, ssm_w_in, ssm_lam_re, ssm_lam_im, ssm_log_dt, ssm_b_re, ssm_b_im, ssm_c_re, ssm_c_im, ssm_d, ssm_w_glu, ssm_b_glu, ssm_w_out, rwkv_mu, rwkv_w_rkvz, rwkv_w0, rwkv_w1, rwkv_w2, rwkv_a0, rwkv_a1, rwkv_a2, rwkv_k_k, rwkv_k_a, rwkv_r_k, rwkv_lnx_g, rwkv_lnx_b, rwkv_w_out, ple_proj, ple_gate, ln_g, ln_b):
    depth = p.shape[0]
    alpha = (2 * depth) ** 0.25
    for i in range(depth):
        kind, j = i % 3, i // 3
        post = (ple_proj[i], ple_gate[i], ln_g[i], ln_b[i], alpha)
        if kind == 0:
            x = _conv_layer(x, p, i, conv_w_in[j], conv_k[j], conv_w_out[j], *post)
        elif kind == 1:
            x = _s5_layer(x, p, i, ssm_w_in[j], ssm_lam_re[j], ssm_lam_im[j], ssm_log_dt[j],
                          ssm_b_re[j], ssm_b_im[j], ssm_c_re[j], ssm_c_im[j], ssm_d[j],
                          ssm_w_glu[j], ssm_b_glu[j], ssm_w_out[j], *post)
        else:
            x = _rwkv_layer(x, p, i, rwkv_mu[j], rwkv_w_rkvz[j], rwkv_w0[j], rwkv_w1[j], rwkv_w2[j],
                            rwkv_a0[j], rwkv_a1[j], rwkv_a2[j], rwkv_k_k[j], rwkv_k_a[j],
                            rwkv_r_k[j].reshape(-1), rwkv_lnx_g[j], rwkv_lnx_b[j], rwkv_w_out[j], *post)
    return x
```

```python
import functools

import jax
import jax.numpy as jnp
from jax import lax
from jax.experimental import pallas as pl
from jax.experimental.pallas import tpu as pltpu

F32 = jnp.float32
BF16 = jnp.bfloat16

LN_EPS = 1e-5
RWKV_GN_EPS = 64e-5
RWKV_HEAD = 64
SSM_GROUP = 16
SUBLANES = 8
LANES = 128
SLAB_GROUPS = LANES // SSM_GROUP
VMEM_LIMIT = 56 * 1024 * 1024

ROW_TILE = 256
CONV_ROW_TILE = 512
CONV_SUBTILES = 4
RWKV_CHUNK = 64
RWKV_BATCH = 4
HEAD_PACK = 4


def _dot(a, b):
    return jnp.dot(a, b, preferred_element_type=F32)


def _split(t, terms):
    pieces = []
    for i in range(terms):
        pieces.append(t.astype(BF16))
        if i + 1 < terms:
            t = t - pieces[-1].astype(F32)
    return pieces


def _silu(z):
    return z * jax.nn.sigmoid(z)


def _shift_rows(cur, prev, shift):
    rolled = pltpu.roll(cur, shift, 0)
    rows = lax.broadcasted_iota(jnp.int32, (SUBLANES, 1), 0)
    top = jnp.where(rows >= shift, rolled[:SUBLANES], pltpu.roll(prev, shift, 0))
    return jnp.concatenate([top, rolled[SUBLANES:]], axis=0)


def _embed(p_ref, wpe_ref):
    return _dot(p_ref[0, 0].astype(BF16), wpe_ref[...])


def _tail(x, y, pe, wpg_ref, lng_ref, lnb_ref, alpha):
    r = alpha * x + y
    r = r + pe * jax.nn.sigmoid(_dot(r.astype(BF16), wpg_ref[...]))
    d = r - jnp.mean(r, axis=-1, keepdims=True)
    var = jnp.mean(d * d, axis=-1, keepdims=True)
    return d * lax.rsqrt(var + LN_EPS) * lng_ref[...] + lnb_ref[...]


def _conv_layer_kernel(x_ref, p_ref, win_ref, ck_ref, wout_ref, wpe_ref, wpg_ref, lng_ref, lnb_ref,
                       o_ref, carry_ref, *, alpha):
    e = wout_ref.shape[0]
    sub = x_ref.shape[1] // CONV_SUBTILES

    @pl.when(pl.program_id(1) == 0)
    def _():
        carry_ref[...] = jnp.zeros_like(carry_ref)

    def mix(i, prev):
        x = x_ref[0, i * sub:(i + 1) * sub, :]
        xb = x.astype(BF16)
        proj = lambda c: _dot(xb, win_ref[:, c * e:(c + 1) * e])
        u = proj(1) * proj(2)
        conv = (ck_ref[0:1, :] * _shift_rows(u, prev, 2) + ck_ref[1:2, :] * _shift_rows(u, prev, 1)
                + ck_ref[2:3, :] * u)
        return x, proj(0) * conv * _silu(proj(3)), u[sub - SUBLANES:, :]

    def finish(i, x, g):
        y = _dot(g.astype(BF16), wout_ref[...])
        pe = _dot(p_ref[0, 0, i * sub:(i + 1) * sub, :].astype(BF16), wpe_ref[...])
        o_ref[0, i * sub:(i + 1) * sub, :] = _tail(x, y, pe, wpg_ref, lng_ref, lnb_ref, alpha)

    prev, pending = carry_ref[...], None
    for i in range(CONV_SUBTILES):
        x, g, prev = mix(i, prev)
        if pending is not None:
            finish(*pending)
        pending = (i, x, g)
    finish(*pending)
    carry_ref[...] = prev


def _full(shape):
    return pl.BlockSpec(shape, lambda b, t: (0,) * len(shape))


def _rows(tm, width):
    return pl.BlockSpec((1, tm, width), lambda b, t: (b, t, 0))


def _layer_rows(layer, tm, width):
    return pl.BlockSpec((1, 1, tm, width), lambda b, t: (layer, b, t, 0))


def _seq_params():
    return pltpu.CompilerParams(dimension_semantics=("arbitrary", "arbitrary"),
                                vmem_limit_bytes=VMEM_LIMIT)


def _conv_layer(x, p, layer, w_in, conv_k, w_out, wpe, wpg, ln_g, ln_b, alpha):
    b, l, d = x.shape
    e = w_out.shape[0]
    tm = min(CONV_ROW_TILE, l)
    return pl.pallas_call(
        functools.partial(_conv_layer_kernel, alpha=alpha),
        grid=(b, l // tm),
        in_specs=[_rows(tm, d), _layer_rows(layer, tm, p.shape[-1]), _full(w_in.shape), _full(conv_k.shape),
                  _full(w_out.shape), _full(wpe.shape), _full(wpg.shape), _full((1, d)), _full((1, d))],
        out_specs=_rows(tm, d),
        out_shape=jax.ShapeDtypeStruct((b, l, d), F32),
        scratch_shapes=[pltpu.VMEM((SUBLANES, e), F32)],
        compiler_params=_seq_params(),
        name="conv_layer",
    )(x, p, w_in.astype(BF16), conv_k, w_out.astype(BF16), wpe.astype(BF16), wpg.astype(BF16),
      ln_g.reshape(1, d), ln_b.reshape(1, d))


def _s5_discretise_kernel(lre_ref, lim_ref, ldt_ref, bre_ref, bim_ref,
                          are_ref, aim_ref, pre_ref, pim_ref, bbre_ref, bbim_ref, *, seg_len):
    lre, lim = lre_ref[...], lim_ref[...]
    dt = jnp.exp(ldt_ref[...])
    mag = jnp.exp(lre * dt)
    are, aim = mag * jnp.cos(lim * dt), mag * jnp.sin(lim * dt)
    den = lre * lre + lim * lim
    cre = ((are - 1.0) * lre + aim * lim) / den
    cim = (aim * lre - (are - 1.0) * lim) / den
    bre, bim = bre_ref[...], bim_ref[...]
    bbre_ref[...] = cre[:, None, :] * bre - cim[:, None, :] * bim
    bbim_ref[...] = cre[:, None, :] * bim + cim[:, None, :] * bre
    are_ref[...] = are
    aim_ref[...] = aim
    sre, sim = are, aim
    for _ in range(seg_len.bit_length() - 1):
        sre, sim = sre * sre - sim * sim, 2.0 * sre * sim
    pr, pi = sre, sim
    for n in range(SUBLANES):
        pre_ref[n] = pr
        pim_ref[n] = pi
        pr, pi = pr * sre - pi * sim, pr * sim + pi * sre


def _s5_layer_kernel(x_ref, p_ref, win_ref, wbre_ref, wbim_ref, wcre_ref, wcim_ref, tab_ref, dsk_ref,
                     wglu_ref, bglu_ref, wout_ref, wpe_ref, wpg_ref, lng_ref, lnb_ref,
                     o_ref, state_ref, sre_ref, sim_ref, y_ref, *, alpha):
    e = wout_ref.shape[0]
    tm = x_ref.shape[1]
    nslab = wbre_ref.shape[0]
    w = wbre_ref.shape[2]
    seg_len = tm // SUBLANES

    @pl.when(pl.program_id(1) == 0)
    def _():
        state_ref[...] = jnp.zeros_like(state_ref)

    x = x_ref[0]
    xb = x.astype(BF16)
    u = _dot(xb, win_ref[:, :e])

    new_r = lax.broadcasted_iota(jnp.int32, (tm, tm), 0)
    old_r = lax.broadcasted_iota(jnp.int32, (tm, tm), 1)
    regroup = jnp.where(old_r == (new_r % SUBLANES) * seg_len + new_r // SUBLANES, 1.0, 0.0).astype(BF16)
    ungroup = jnp.where(new_r == (old_r % SUBLANES) * seg_len + old_r // SUBLANES, 1.0, 0.0).astype(BF16)
    ub = _dot(regroup, u.astype(BF16)).astype(BF16)
    rows = lax.broadcasted_iota(jnp.int32, (SUBLANES, 1), 0)

    def project_in(s):
        us = ub[:, s * LANES:(s + 1) * LANES]
        sre_ref[s] = _dot(us, wbre_ref[s])
        sim_ref[s] = _dot(us, wbim_ref[s])

    z_parts = []
    z_w = 2 * e // nslab
    project_in(0)
    for s in range(nslab):
        if s + 1 < nslab:
            project_in(s + 1)
        if s % 2 == 0:
            z_parts.append(_dot(xb, win_ref[:, e + (s // 2) * z_w:e + (s // 2 + 1) * z_w]))
        if s == 1:
            pe = _embed(p_ref, wpe_ref)
        sre, sim = sre_ref.at[s], sim_ref.at[s]
        ar, ai = tab_ref[s, 0], tab_ref[s, 1]

        def local_final(i, c):
            cr, ci = c
            r0 = pl.multiple_of(i * SUBLANES, SUBLANES)
            return (cr * ar - ci * ai + sre[pl.ds(r0, SUBLANES), :],
                    cr * ai + ci * ar + sim[pl.ds(r0, SUBLANES), :])

        zero = jnp.zeros((SUBLANES, w), F32)
        fr, fi = lax.fori_loop(0, seg_len, local_final, (zero, zero), unroll=True)
        for lvl, dist in enumerate((1, 2, 4)):
            pr, pi = tab_ref[s, 2 + 2 * lvl], tab_ref[s, 3 + 2 * lvl]
            gr, gi = pltpu.roll(fr, dist, 0), pltpu.roll(fi, dist, 0)
            fr, fi = fr + pr * gr - pi * gi, fi + pr * gi + pi * gr
        cr0, ci0 = state_ref[2 * s:2 * s + 1, :], state_ref[2 * s + 1:2 * s + 2, :]
        qr, qi = tab_ref[s, 8], tab_ref[s, 9]
        init_r = qr * cr0 - qi * ci0 + jnp.where(rows >= 1, pltpu.roll(fr, 1, 0), 0.0)
        init_i = qr * ci0 + qi * cr0 + jnp.where(rows >= 1, pltpu.roll(fi, 1, 0), 0.0)

        def all_states(i, c):
            cr, ci = c
            r0 = pl.multiple_of(i * SUBLANES, SUBLANES)
            nr = cr * ar - ci * ai + sre[pl.ds(r0, SUBLANES), :]
            ni = cr * ai + ci * ar + sim[pl.ds(r0, SUBLANES), :]
            sre[pl.ds(r0, SUBLANES), :] = nr
            sim[pl.ds(r0, SUBLANES), :] = ni
            return nr, ni

        lr, li = lax.fori_loop(0, seg_len, all_states, (init_r, init_i), unroll=True)
        state_ref[2 * s:2 * s + 1, :] = lr[SUBLANES - 1:, :]
        state_ref[2 * s + 1:2 * s + 2, :] = li[SUBLANES - 1:, :]
        y_ref[:, s * LANES:(s + 1) * LANES] = (_dot(sre[...].astype(BF16), wcre_ref[s])
                                               + _dot(sim[...].astype(BF16), wcim_ref[s]))

    y = sum(_dot(ungroup, piece) for piece in _split(y_ref[...], 2))

    y = jax.nn.gelu(y + dsk_ref[...] * u)
    y = y * jax.nn.sigmoid(_dot(y.astype(BF16), wglu_ref[...]) + bglu_ref[...])
    z = jnp.concatenate(z_parts, axis=-1)
    out = _dot((y * _silu(z)).astype(BF16), wout_ref[...])
    o_ref[0] = _tail(x, out, pe, wpg_ref, lng_ref, lnb_ref, alpha)


def _s5_layer(x, p, layer, w_in, lam_re, lam_im, log_dt, b_re, b_im, c_re, c_im, d_skip, w_glu, b_glu, w_out,
              wpe, wpg, ln_g, ln_b, alpha):
    b, l, d = x.shape
    e = w_out.shape[0]
    g, pst = lam_re.shape
    hch = b_re.shape[-1]
    nslab = g // SLAB_GROUPS
    w = SLAB_GROUPS * pst
    tm = min(ROW_TILE, l)
    seg_len = tm // SUBLANES
    assert seg_len & (seg_len - 1) == 0, "segment length must be a power of two"

    gp1 = jax.ShapeDtypeStruct((g, pst), F32)
    gp = jax.ShapeDtypeStruct((SUBLANES, g, pst), F32)
    gb = jax.ShapeDtypeStruct((g, hch, pst), F32)
    a_re, a_im, pw_re, pw_im, bb_re, bb_im = pl.pallas_call(
        functools.partial(_s5_discretise_kernel, seg_len=seg_len),
        out_shape=(gp1, gp1, gp, gp, gb, gb), name="s5_discretise",
    )(lam_re, lam_im, log_dt.reshape(g, 1), jnp.swapaxes(b_re, 1, 2), jnp.swapaxes(b_im, 1, 2))

    eye = jnp.eye(SLAB_GROUPS, dtype=F32)
    blk_in = lambda m: jnp.einsum("sghp,gk->sghkp", m.reshape(nslab, SLAB_GROUPS, hch, pst),
                                  eye).reshape(nslab, LANES, w).astype(BF16)
    blk_out = lambda m: jnp.einsum("sghp,gk->skpgh", m.reshape(nslab, SLAB_GROUPS, hch, pst),
                                   eye).reshape(nslab, w, LANES).astype(BF16)
    rows = jnp.arange(SUBLANES)[:, None, None]
    flat = lambda m: m.reshape(SUBLANES, nslab, w)
    tabs = [jnp.broadcast_to(m.reshape(1, nslab, w), (SUBLANES, nslab, w)) for m in (a_re, a_im)]
    for dist in (1, 2, 4):
        tabs += [jnp.where(rows >= dist, flat(m)[dist - 1][None], 0.0) for m in (pw_re, pw_im)]
    tabs.append(jnp.concatenate([jnp.ones((1, nslab, w), F32), flat(pw_re)[:SUBLANES - 1]]))
    tabs.append(jnp.concatenate([jnp.zeros((1, nslab, w), F32), flat(pw_im)[:SUBLANES - 1]]))
    tab = jnp.transpose(jnp.stack(tabs), (2, 0, 1, 3))

    return pl.pallas_call(
        functools.partial(_s5_layer_kernel, alpha=alpha),
        grid=(b, l // tm),
        in_specs=[_rows(tm, d), _layer_rows(layer, tm, p.shape[-1]), _full(w_in.shape),
                  _full((nslab, LANES, w)), _full((nslab, LANES, w)),
                  _full((nslab, w, LANES)), _full((nslab, w, LANES)),
                  _full(tab.shape), _full((1, e)),
                  _full(w_glu.shape), _full((1, e)), _full(w_out.shape), _full(wpe.shape),
                  _full(wpg.shape), _full((1, d)), _full((1, d))],
        out_specs=_rows(tm, d),
        out_shape=jax.ShapeDtypeStruct((b, l, d), F32),
        scratch_shapes=[pltpu.VMEM((2 * nslab, w), F32), pltpu.VMEM((nslab, tm, w), F32),
                        pltpu.VMEM((nslab, tm, w), F32), pltpu.VMEM((tm, e), F32)],
        compiler_params=_seq_params(),
        name="s5_layer",
    )(x, p, w_in.astype(BF16), blk_in(bb_re), blk_in(bb_im),
      blk_out(c_re), blk_out(-c_im),
      tab, d_skip.reshape(1, e), w_glu.astype(BF16), b_glu.reshape(1, e), w_out.astype(BF16),
      wpe.astype(BF16), wpg.astype(BF16), ln_g.reshape(1, d), ln_b.reshape(1, d))


def _rwkv_layer_kernel(x_ref, p_ref, mu_ref, wrkvz_ref, w0_ref, w1_ref, w2_ref, a0_ref, a1_ref, a2_ref,
                       kk_ref, ka_ref, rk_ref, lgx_ref, lbx_ref, wout_ref, wpe_ref, wpg_ref, lng_ref, lnb_ref,
                       o_ref, s_ref, carry_ref, *, alpha):
    nb, tc, d = x_ref.shape
    e = wout_ref.shape[0]
    n = RWKV_HEAD
    rows = nb * tc

    @pl.when(pl.program_id(1) == 0)
    def _():
        s_ref[...] = jnp.zeros_like(s_ref)
        carry_ref[...] = jnp.zeros_like(carry_ref)

    x3 = x_ref[...]
    x = x3.reshape(rows, d)
    shifted = jnp.concatenate([_shift_rows(x3[b], carry_ref[b], 1) for b in range(nb)], axis=0)
    carry_ref[...] = x3[:, tc - SUBLANES:, :]
    dx = shifted - x
    mix = lambda i: (x + dx * mu_ref[i:i + 1, :]).astype(BF16)
    r, k, v, z = (_dot(mix(i), wrkvz_ref[i]) for i in range(4))
    lora_w = _dot(jnp.tanh(_dot(mix(4), w1_ref[...])).astype(BF16), w2_ref[...])
    w_log = -jax.nn.softplus(-(w0_ref[...] + lora_w)) - 0.5
    lw = -jnp.exp(w_log)
    lora_a = _dot(_dot(mix(5), a1_ref[...]).astype(BF16), a2_ref[...])
    a = jax.nn.sigmoid(a0_ref[...] + lora_a)

    sw = HEAD_PACK * n
    lane_h = lax.broadcasted_iota(jnp.int32, (sw, sw), 1) // n
    diag = lax.broadcasted_iota(jnp.int32, (sw, sw), 0) // n == lane_h
    t_row = lax.broadcasted_iota(jnp.int32, (tc, sw), 0)
    t_col = lax.broadcasted_iota(jnp.int32, (tc, sw), 1) % n
    strict, incl, eye = t_row > t_col, t_row >= t_col, jnp.where(t_row == t_col, 1.0, 0.0)
    dot_nt = lambda lhs, rhs: lax.dot_general(lhs, rhs, (((1,), (1,)), ((), ())), preferred_element_type=F32)
    dot_tn = lambda lhs, rhs: lax.dot_general(lhs, rhs, (((0,), (0,)), ((), ())), preferred_element_type=F32)

    def blocks(m):
        return jnp.where(diag, jnp.concatenate([m] * HEAD_PACK, axis=0), jnp.zeros((), m.dtype))

    seg_ones = jnp.where(diag, 1.0, 0.0).astype(BF16)

    def seg_sum(t, terms):
        pieces = _split(t, terms)
        return jnp.concatenate([sum(_dot(piece[:, j:j + sw], seg_ones) for piece in pieces)
                                for j in range(0, e, sw)], axis=-1)

    brow = lax.broadcasted_iota(jnp.int32, (rows, rows), 0)
    bcol = lax.broadcasted_iota(jnp.int32, (rows, rows), 1)
    tri = jnp.where((brow >= bcol) & (brow // tc == bcol // tc), 1.0, 0.0).astype(BF16)
    g = sum(_dot(tri, piece) for piece in _split(lw, 2))
    g_end = g.reshape(nb, tc, e)[:, tc - 1:, :]

    kmod = k * (1.0 + (a - 1.0) * ka_ref[...])
    e_neg = jnp.exp(-g)
    e_end = jnp.exp(g_end - g.reshape(nb, tc, e)).reshape(rows, e)
    kkr = k * kk_ref[...]
    kk = kkr / jnp.maximum(jnp.sqrt(seg_sum(kkr * kkr, 2)), 1e-12)
    bvec = kk * a
    at2, rt2 = (-kk * jnp.exp(g - lw)).astype(BF16), (r * jnp.exp(g)).astype(BF16)
    bt2, kt2 = (bvec * e_neg).astype(BF16), (kmod * e_neg).astype(BF16)
    bh2, kh2 = (bvec * e_end).astype(BF16), (kmod * e_end).astype(BF16)
    v2 = v.astype(BF16)
    decay = jnp.exp(g_end)

    units = [(b, j) for b in range(nb) for j in range(e // sw)]
    cut = lambda t, u: t[u[0] * tc:(u[0] + 1) * tc, u[1] * sw:(u[1] + 1) * sw]
    each = lambda fn, *lists: [fn(*xs) for xs in zip(*lists)]
    at, rt, vv = ([cut(t, u) for u in units] for t in (at2, rt2, v2))
    s0 = [s_ref[b, j] for b, j in units]
    s0b = [t.astype(BF16) for t in s0]
    lhs = each(lambda x, y: jnp.concatenate([x, y], axis=0), at, rt)
    aab = each(lambda x, u: dot_nt(x, blocks(cut(bt2, u))), lhs, units)
    aak = each(lambda x, u: dot_nt(x, blocks(cut(kt2, u))), lhs, units)
    a_ab = [jnp.where(strict, t[:tc], 0.0).astype(BF16) for t in aab]
    a_ak = [jnp.where(strict, t[:tc], 0.0).astype(BF16) for t in aak]
    a_r = each(lambda x, y: jnp.concatenate([jnp.where(incl, x[tc:], 0.0), jnp.where(incl, y[tc:], 0.0)],
                                            axis=1).astype(BF16), aab, aak)
    v_blocks = [blocks(t) for t in vv]
    rhs = each(lambda x, s, y, vb: dot_nt(x, s) + _dot(y, vb), at, s0b, a_ak, v_blocks)
    apow = a_ab
    inv = [eye + t.astype(F32) for t in apow]
    span = 2
    while span < tc:
        apow = [_dot(t, blocks(t)).astype(BF16) for t in apow]
        inv = each(lambda x, y: x + _dot(y, blocks(x.astype(BF16))), inv, apow)
        span *= 2
    sa = each(lambda x, y: _dot(x.astype(BF16), blocks(y.astype(BF16))).astype(BF16), inv, rhs)
    out = each(lambda x, s, y, z, vb: dot_nt(x, s) + _dot(y, jnp.concatenate([blocks(z), vb], axis=0)),
               rt, s0b, a_r, sa, v_blocks)
    update = each(lambda x, y, u: dot_tn(jnp.concatenate([x, y], axis=0),
                                         jnp.concatenate([cut(bh2, u), cut(kh2, u)], axis=0)),
                  sa, vv, units)
    for (b, j), s, upd in zip(units, s0, update):
        s_ref[b, j] = s * decay[b][:, j * sw:(j + 1) * sw] + jnp.where(diag, upd, 0.0)

    nslab = e // sw
    out = jnp.concatenate(
        [jnp.concatenate(out[b * nslab:(b + 1) * nslab], axis=1) for b in range(nb)], axis=0)
    dev = out - seg_sum(out, 1) * (1.0 / n)
    var = seg_sum(dev * dev, 1) * (1.0 / n)
    bonus = seg_sum(r * kmod * rk_ref[...], 1) * v
    res = dev * lax.rsqrt(var + RWKV_GN_EPS) * lgx_ref[...] + lbx_ref[...] + bonus

    y = _dot((res * _silu(z)).astype(BF16), wout_ref[...])
    pe = _dot(p_ref[0].reshape(rows, p_ref.shape[-1]).astype(BF16), wpe_ref[...])
    o_ref[...] = _tail(x, y, pe, wpg_ref, lng_ref, lnb_ref, alpha).reshape(nb, tc, d)


def _rwkv_layer(x, p, layer, mu, w_rkvz, w0, w1, w2, a0, a1, a2, k_k, k_a, r_k, lnx_g, lnx_b, w_out,
                wpe, wpg, ln_g, ln_b, alpha):
    b, l, d = x.shape
    e = w_out.shape[0]
    slab = HEAD_PACK * RWKV_HEAD
    tc = min(RWKV_CHUNK, l)
    nb = RWKV_BATCH if b % RWKV_BATCH == 0 else 1
    row1 = lambda t: t.reshape(1, -1)
    chunk_rows = lambda width: pl.BlockSpec((nb, tc, width), lambda i, t: (i, t, 0))
    return pl.pallas_call(
        functools.partial(_rwkv_layer_kernel, alpha=alpha),
        grid=(b // nb, l // tc),
        in_specs=[chunk_rows(d), pl.BlockSpec((1, nb, tc, p.shape[-1]), lambda i, t: (layer, i, t, 0)),
                  _full(mu.shape), _full(w_rkvz.shape), _full((1, e)), _full(w1.shape), _full(w2.shape),
                  _full((1, e)), _full(a1.shape), _full(a2.shape)] + [_full((1, e))] * 5
                 + [_full(w_out.shape), _full(wpe.shape), _full(wpg.shape), _full((1, d)), _full((1, d))],
        out_specs=chunk_rows(d),
        out_shape=jax.ShapeDtypeStruct((b, l, d), F32),
        scratch_shapes=[pltpu.VMEM((nb, e // slab, slab, slab), F32),
                        pltpu.VMEM((nb, SUBLANES, d), F32)],
        compiler_params=_seq_params(),
        name="rwkv_layer",
    )(x, p, mu, w_rkvz.astype(BF16), row1(w0), w1.astype(BF16), w2.astype(BF16), row1(a0),
      a1.astype(BF16), a2.astype(BF16), row1(k_k), row1(k_a), row1(r_k), row1(lnx_g), row1(lnx_b),
      w_out.astype(BF16), wpe.astype(BF16), wpg.astype(BF16), row1(ln_g), row1(ln_b))


def kernel(x, p, conv_w_in, conv_k, conv_w_out, ssm_w_in, ssm_lam_re, ssm_lam_im, ssm_log_dt, ssm_b_re, ssm_b_im, ssm_c_re, ssm_c_im, ssm_d, ssm_w_glu, ssm_b_glu, ssm_w_out, rwkv_mu, rwkv_w_rkvz, rwkv_w0, rwkv_w1, rwkv_w2, rwkv_a0, rwkv_a1, rwkv_a2, rwkv_k_k, rwkv_k_a, rwkv_r_k, rwkv_lnx_g, rwkv_lnx_b, rwkv_w_out, ple_proj, ple_gate, ln_g, ln_b):
    depth = p.shape[0]
    alpha = (2 * depth) ** 0.25
    for i in range(depth):
        kind, j = i % 3, i // 3
        post = (ple_proj[i], ple_gate[i], ln_g[i], ln_b[i], alpha)
        if kind == 0:
            x = _conv_layer(x, p, i, conv_w_in[j], conv_k[j], conv_w_out[j], *post)
        elif kind == 1:
            x = _s5_layer(x, p, i, ssm_w_in[j], ssm_lam_re[j], ssm_lam_im[j], ssm_log_dt[j],
                          ssm_b_re[j], ssm_b_im[j], ssm_c_re[j], ssm_c_im[j], ssm_d[j],
                          ssm_w_glu[j], ssm_b_glu[j], ssm_w_out[j], *post)
        else:
            x = _rwkv_layer(x, p, i, rwkv_mu[j], rwkv_w_rkvz[j], rwkv_w0[j], rwkv_w1[j], rwkv_w2[j],
                            rwkv_a0[j], rwkv_a1[j], rwkv_a2[j], rwkv_k_k[j], rwkv_k_a[j],
                            rwkv_r_k[j].reshape(-1), rwkv_lnx_g[j], rwkv_lnx_b[j], rwkv_w_out[j], *post)
    return x
```

```python
import functools

import jax
import jax.numpy as jnp
from jax import lax
from jax.experimental import pallas as pl
from jax.experimental.pallas import tpu as pltpu

F32 = jnp.float32
BF16 = jnp.bfloat16

LN_EPS = 1e-5
RWKV_GN_EPS = 64e-5
RWKV_HEAD = 64
SSM_GROUP = 16
SUBLANES = 8
LANES = 128
SLAB_GROUPS = LANES // SSM_GROUP
VMEM_LIMIT = 56 * 1024 * 1024

ROW_TILE = 256
CONV_ROW_TILE = 512
CONV_SUBTILES = 4
RWKV_CHUNK = 64
RWKV_BATCH = 4


def _dot(a, b):
    return jnp.dot(a, b, preferred_element_type=F32)


def _split(t, terms):
    pieces = []
    for i in range(terms):
        pieces.append(t.astype(BF16))
        if i + 1 < terms:
            t = t - pieces[-1].astype(F32)
    return pieces


def _silu(z):
    return z * jax.nn.sigmoid(z)


def _shift_rows(cur, prev, shift):
    rolled = pltpu.roll(cur, shift, 0)
    rows = lax.broadcasted_iota(jnp.int32, (SUBLANES, 1), 0)
    top = jnp.where(rows >= shift, rolled[:SUBLANES], pltpu.roll(prev, shift, 0))
    return jnp.concatenate([top, rolled[SUBLANES:]], axis=0)


def _embed(p_ref, wpe_ref):
    return _dot(p_ref[0, 0].astype(BF16), wpe_ref[...])


def _tail(x, y, pe, wpg_ref, lng_ref, lnb_ref, alpha):
    r = alpha * x + y
    r = r + pe * jax.nn.sigmoid(_dot(r.astype(BF16), wpg_ref[...]))
    d = r - jnp.mean(r, axis=-1, keepdims=True)
    var = jnp.mean(d * d, axis=-1, keepdims=True)
    return d * lax.rsqrt(var + LN_EPS) * lng_ref[...] + lnb_ref[...]


def _conv_layer_kernel(x_ref, p_ref, win_ref, ck_ref, wout_ref, wpe_ref, wpg_ref, lng_ref, lnb_ref,
                       o_ref, carry_ref, *, alpha):
    e = wout_ref.shape[0]
    sub = x_ref.shape[1] // CONV_SUBTILES

    @pl.when(pl.program_id(1) == 0)
    def _():
        carry_ref[...] = jnp.zeros_like(carry_ref)

    def mix(i, prev):
        x = x_ref[0, i * sub:(i + 1) * sub, :]
        xb = x.astype(BF16)
        proj = lambda c: _dot(xb, win_ref[:, c * e:(c + 1) * e])
        u = proj(1) * proj(2)
        conv = (ck_ref[0:1, :] * _shift_rows(u, prev, 2) + ck_ref[1:2, :] * _shift_rows(u, prev, 1)
                + ck_ref[2:3, :] * u)
        return x, proj(0) * conv * _silu(proj(3)), u[sub - SUBLANES:, :]

    def finish(i, x, g):
        y = _dot(g.astype(BF16), wout_ref[...])
        pe = _dot(p_ref[0, 0, i * sub:(i + 1) * sub, :].astype(BF16), wpe_ref[...])
        o_ref[0, i * sub:(i + 1) * sub, :] = _tail(x, y, pe, wpg_ref, lng_ref, lnb_ref, alpha)

    prev, pending = carry_ref[...], None
    for i in range(CONV_SUBTILES):
        x, g, prev = mix(i, prev)
        if pending is not None:
            finish(*pending)
        pending = (i, x, g)
    finish(*pending)
    carry_ref[...] = prev


def _full(shape):
    return pl.BlockSpec(shape, lambda b, t: (0,) * len(shape))


def _rows(tm, width):
    return pl.BlockSpec((1, tm, width), lambda b, t: (b, t, 0))


def _layer_rows(layer, tm, width):
    return pl.BlockSpec((1, 1, tm, width), lambda b, t: (layer, b, t, 0))


def _seq_params():
    return pltpu.CompilerParams(dimension_semantics=("arbitrary", "arbitrary"),
                                vmem_limit_bytes=VMEM_LIMIT)


def _conv_layer(x, p, layer, w_in, conv_k, w_out, wpe, wpg, ln_g, ln_b, alpha):
    b, l, d = x.shape
    e = w_out.shape[0]
    tm = min(CONV_ROW_TILE, l)
    return pl.pallas_call(
        functools.partial(_conv_layer_kernel, alpha=alpha),
        grid=(b, l // tm),
        in_specs=[_rows(tm, d), _layer_rows(layer, tm, p.shape[-1]), _full(w_in.shape), _full(conv_k.shape),
                  _full(w_out.shape), _full(wpe.shape), _full(wpg.shape), _full((1, d)), _full((1, d))],
        out_specs=_rows(tm, d),
        out_shape=jax.ShapeDtypeStruct((b, l, d), F32),
        scratch_shapes=[pltpu.VMEM((SUBLANES, e), F32)],
        compiler_params=_seq_params(),
        name="conv_layer",
    )(x, p, w_in.astype(BF16), conv_k, w_out.astype(BF16), wpe.astype(BF16), wpg.astype(BF16),
      ln_g.reshape(1, d), ln_b.reshape(1, d))


def _s5_discretise_kernel(lre_ref, lim_ref, ldt_ref, bre_ref, bim_ref,
                          are_ref, aim_ref, pre_ref, pim_ref, bbre_ref, bbim_ref, *, seg_len):
    lre, lim = lre_ref[...], lim_ref[...]
    dt = jnp.exp(ldt_ref[...])
    mag = jnp.exp(lre * dt)
    are, aim = mag * jnp.cos(lim * dt), mag * jnp.sin(lim * dt)
    den = lre * lre + lim * lim
    cre = ((are - 1.0) * lre + aim * lim) / den
    cim = (aim * lre - (are - 1.0) * lim) / den
    bre, bim = bre_ref[...], bim_ref[...]
    bbre_ref[...] = cre[:, None, :] * bre - cim[:, None, :] * bim
    bbim_ref[...] = cre[:, None, :] * bim + cim[:, None, :] * bre
    are_ref[...] = are
    aim_ref[...] = aim
    sre, sim = are, aim
    for _ in range(seg_len.bit_length() - 1):
        sre, sim = sre * sre - sim * sim, 2.0 * sre * sim
    pr, pi = sre, sim
    for n in range(SUBLANES):
        pre_ref[n] = pr
        pim_ref[n] = pi
        pr, pi = pr * sre - pi * sim, pr * sim + pi * sre


def _s5_layer_kernel(x_ref, p_ref, win_ref, wbre_ref, wbim_ref, wcre_ref, wcim_ref, tab_ref, dsk_ref,
                     wglu_ref, bglu_ref, wout_ref, wpe_ref, wpg_ref, lng_ref, lnb_ref,
                     o_ref, state_ref, sre_ref, sim_ref, y_ref, *, alpha):
    e = wout_ref.shape[0]
    tm = x_ref.shape[1]
    nslab = wbre_ref.shape[0]
    w = wbre_ref.shape[2]
    seg_len = tm // SUBLANES

    @pl.when(pl.program_id(1) == 0)
    def _():
        state_ref[...] = jnp.zeros_like(state_ref)

    x = x_ref[0]
    xb = x.astype(BF16)
    u = _dot(xb, win_ref[:, :e])

    new_r = lax.broadcasted_iota(jnp.int32, (tm, tm), 0)
    old_r = lax.broadcasted_iota(jnp.int32, (tm, tm), 1)
    regroup = jnp.where(old_r == (new_r % SUBLANES) * seg_len + new_r // SUBLANES, 1.0, 0.0).astype(BF16)
    ungroup = jnp.where(new_r == (old_r % SUBLANES) * seg_len + old_r // SUBLANES, 1.0, 0.0).astype(BF16)
    ub = _dot(regroup, u.astype(BF16)).astype(BF16)
    rows = lax.broadcasted_iota(jnp.int32, (SUBLANES, 1), 0)

    def project_in(s):
        us = ub[:, s * LANES:(s + 1) * LANES]
        sre_ref[s] = _dot(us, wbre_ref[s])
        sim_ref[s] = _dot(us, wbim_ref[s])

    z_parts = []
    z_w = 2 * e // nslab
    project_in(0)
    for s in range(nslab):
        if s + 1 < nslab:
            project_in(s + 1)
        if s % 2 == 0:
            z_parts.append(_dot(xb, win_ref[:, e + (s // 2) * z_w:e + (s // 2 + 1) * z_w]))
        if s == 1:
            pe = _embed(p_ref, wpe_ref)
        sre, sim = sre_ref.at[s], sim_ref.at[s]
        ar, ai = tab_ref[s, 0], tab_ref[s, 1]

        def local_final(i, c):
            cr, ci = c
            r0 = pl.multiple_of(i * SUBLANES, SUBLANES)
            return (cr * ar - ci * ai + sre[pl.ds(r0, SUBLANES), :],
                    cr * ai + ci * ar + sim[pl.ds(r0, SUBLANES), :])

        zero = jnp.zeros((SUBLANES, w), F32)
        fr, fi = lax.fori_loop(0, seg_len, local_final, (zero, zero), unroll=True)
        for lvl, dist in enumerate((1, 2, 4)):
            pr, pi = tab_ref[s, 2 + 2 * lvl], tab_ref[s, 3 + 2 * lvl]
            gr, gi = pltpu.roll(fr, dist, 0), pltpu.roll(fi, dist, 0)
            fr, fi = fr + pr * gr - pi * gi, fi + pr * gi + pi * gr
        cr0, ci0 = state_ref[2 * s:2 * s + 1, :], state_ref[2 * s + 1:2 * s + 2, :]
        qr, qi = tab_ref[s, 8], tab_ref[s, 9]
        init_r = qr * cr0 - qi * ci0 + jnp.where(rows >= 1, pltpu.roll(fr, 1, 0), 0.0)
        init_i = qr * ci0 + qi * cr0 + jnp.where(rows >= 1, pltpu.roll(fi, 1, 0), 0.0)

        def all_states(i, c):
            cr, ci = c
            r0 = pl.multiple_of(i * SUBLANES, SUBLANES)
            nr = cr * ar - ci * ai + sre[pl.ds(r0, SUBLANES), :]
            ni = cr * ai + ci * ar + sim[pl.ds(r0, SUBLANES), :]
            sre[pl.ds(r0, SUBLANES), :] = nr
            sim[pl.ds(r0, SUBLANES), :] = ni
            return nr, ni

        lr, li = lax.fori_loop(0, seg_len, all_states, (init_r, init_i), unroll=True)
        state_ref[2 * s:2 * s + 1, :] = lr[SUBLANES - 1:, :]
        state_ref[2 * s + 1:2 * s + 2, :] = li[SUBLANES - 1:, :]
        y_ref[:, s * LANES:(s + 1) * LANES] = (_dot(sre[...].astype(BF16), wcre_ref[s])
                                               + _dot(sim[...].astype(BF16), wcim_ref[s]))

    y = sum(_dot(ungroup, piece) for piece in _split(y_ref[...], 2))

    y = jax.nn.gelu(y + dsk_ref[...] * u)
    y = y * jax.nn.sigmoid(_dot(y.astype(BF16), wglu_ref[...]) + bglu_ref[...])
    z = jnp.concatenate(z_parts, axis=-1)
    out = _dot((y * _silu(z)).astype(BF16), wout_ref[...])
    o_ref[0] = _tail(x, out, pe, wpg_ref, lng_ref, lnb_ref, alpha)


def _s5_layer(x, p, layer, w_in, lam_re, lam_im, log_dt, b_re, b_im, c_re, c_im, d_skip, w_glu, b_glu, w_out,
              wpe, wpg, ln_g, ln_b, alpha):
    b, l, d = x.shape
    e = w_out.shape[0]
    g, pst = lam_re.shape
    hch = b_re.shape[-1]
    nslab = g // SLAB_GROUPS
    w = SLAB_GROUPS * pst
    tm = min(ROW_TILE, l)
    seg_len = tm // SUBLANES
    assert seg_len & (seg_len - 1) == 0, "segment length must be a power of two"

    gp1 = jax.ShapeDtypeStruct((g, pst), F32)
    gp = jax.ShapeDtypeStruct((SUBLANES, g, pst), F32)
    gb = jax.ShapeDtypeStruct((g, hch, pst), F32)
    a_re, a_im, pw_re, pw_im, bb_re, bb_im = pl.pallas_call(
        functools.partial(_s5_discretise_kernel, seg_len=seg_len),
        out_shape=(gp1, gp1, gp, gp, gb, gb), name="s5_discretise",
    )(lam_re, lam_im, log_dt.reshape(g, 1), jnp.swapaxes(b_re, 1, 2), jnp.swapaxes(b_im, 1, 2))

    eye = jnp.eye(SLAB_GROUPS, dtype=F32)
    blk_in = lambda m: jnp.einsum("sghp,gk->sghkp", m.reshape(nslab, SLAB_GROUPS, hch, pst),
                                  eye).reshape(nslab, LANES, w).astype(BF16)
    blk_out = lambda m: jnp.einsum("sghp,gk->skpgh", m.reshape(nslab, SLAB_GROUPS, hch, pst),
                                   eye).reshape(nslab, w, LANES).astype(BF16)
    rows = jnp.arange(SUBLANES)[:, None, None]
    flat = lambda m: m.reshape(SUBLANES, nslab, w)
    tabs = [jnp.broadcast_to(m.reshape(1, nslab, w), (SUBLANES, nslab, w)) for m in (a_re, a_im)]
    for dist in (1, 2, 4):
        tabs += [jnp.where(rows >= dist, flat(m)[dist - 1][None], 0.0) for m in (pw_re, pw_im)]
    tabs.append(jnp.concatenate([jnp.ones((1, nslab, w), F32), flat(pw_re)[:SUBLANES - 1]]))
    tabs.append(jnp.concatenate([jnp.zeros((1, nslab, w), F32), flat(pw_im)[:SUBLANES - 1]]))
    tab = jnp.transpose(jnp.stack(tabs), (2, 0, 1, 3))

    return pl.pallas_call(
        functools.partial(_s5_layer_kernel, alpha=alpha),
        grid=(b, l // tm),
        in_specs=[_rows(tm, d), _layer_rows(layer, tm, p.shape[-1]), _full(w_in.shape),
                  _full((nslab, LANES, w)), _full((nslab, LANES, w)),
                  _full((nslab, w, LANES)), _full((nslab, w, LANES)),
                  _full(tab.shape), _full((1, e)),
                  _full(w_glu.shape), _full((1, e)), _full(w_out.shape), _full(wpe.shape),
                  _full(wpg.shape), _full((1, d)), _full((1, d))],
        out_specs=_rows(tm, d),
        out_shape=jax.ShapeDtypeStruct((b, l, d), F32),
        scratch_shapes=[pltpu.VMEM((2 * nslab, w), F32), pltpu.VMEM((nslab, tm, w), F32),
                        pltpu.VMEM((nslab, tm, w), F32), pltpu.VMEM((tm, e), F32)],
        compiler_params=_seq_params(),
        name="s5_layer",
    )(x, p, w_in.astype(BF16), blk_in(bb_re), blk_in(bb_im),
      blk_out(c_re), blk_out(-c_im),
      tab, d_skip.reshape(1, e), w_glu.astype(BF16), b_glu.reshape(1, e), w_out.astype(BF16),
      wpe.astype(BF16), wpg.astype(BF16), ln_g.reshape(1, d), ln_b.reshape(1, d))


def _rwkv_layer_kernel(x_ref, p_ref, mu_ref, wrkvz_ref, w0_ref, w1_ref, w2_ref, a0_ref, a1_ref, a2_ref,
                       kk_ref, ka_ref, rk_ref, lgx_ref, lbx_ref, wout_ref, wpe_ref, wpg_ref, lng_ref, lnb_ref,
                       o_ref, s_ref, carry_ref, *, alpha):
    nb, tc, d = x_ref.shape
    e = wout_ref.shape[0]
    n = s_ref.shape[1]
    nh = s_ref.shape[0] // nb
    rows = nb * tc

    @pl.when(pl.program_id(1) == 0)
    def _():
        s_ref[...] = jnp.zeros_like(s_ref)
        carry_ref[...] = jnp.zeros_like(carry_ref)

    x3 = x_ref[...]
    x = x3.reshape(rows, d)
    shifted = jnp.concatenate([_shift_rows(x3[b], carry_ref[b], 1) for b in range(nb)], axis=0)
    carry_ref[...] = x3[:, tc - SUBLANES:, :]
    dx = shifted - x
    mix = lambda i: (x + dx * mu_ref[i:i + 1, :]).astype(BF16)
    r, k, v, z = (_dot(mix(i), wrkvz_ref[i]) for i in range(4))
    lora_w = _dot(jnp.tanh(_dot(mix(4), w1_ref[...])).astype(BF16), w2_ref[...])
    w_log = -jax.nn.softplus(-(w0_ref[...] + lora_w)) - 0.5
    lw = -jnp.exp(w_log)
    lora_a = _dot(_dot(mix(5), a1_ref[...]).astype(BF16), a2_ref[...])
    a = jax.nn.sigmoid(a0_ref[...] + lora_a)

    heads = lambda t: jnp.stack([t[(t.shape[0] // nb) * b:(t.shape[0] // nb) * (b + 1), n * h:n * (h + 1)]
                                 for b in range(nb) for h in range(nh)])
    row = lax.broadcasted_iota(jnp.int32, (tc, tc), 0)
    col = lax.broadcasted_iota(jnp.int32, (tc, tc), 1)

    brow = lax.broadcasted_iota(jnp.int32, (rows, rows), 0)
    bcol = lax.broadcasted_iota(jnp.int32, (rows, rows), 1)
    tri = jnp.where((brow >= bcol) & (brow // tc == bcol // tc), 1.0, 0.0).astype(BF16)
    g = sum(_dot(tri, piece) for piece in _split(lw, 2))
    g_end = g.reshape(nb, tc, e)[:, tc - 1:, :]

    seg_w = 2 * LANES
    srow = lax.broadcasted_iota(jnp.int32, (seg_w, seg_w), 0)
    scol = lax.broadcasted_iota(jnp.int32, (seg_w, seg_w), 1)
    seg_ones = jnp.where(srow // n == scol // n, 1.0, 0.0).astype(BF16)

    def seg_sum(t, terms):
        pieces = _split(t, terms)
        return jnp.concatenate([sum(_dot(piece[:, j:j + seg_w], seg_ones) for piece in pieces)
                                for j in range(0, e, seg_w)], axis=-1)

    kmod = k * (1.0 + (a - 1.0) * ka_ref[...])
    e_neg = jnp.exp(-g)
    e_end = jnp.exp(g_end - g.reshape(nb, tc, e)).reshape(rows, e)
    kkr = k * kk_ref[...]
    kk = kkr / jnp.maximum(jnp.sqrt(seg_sum(kkr * kkr, 2)), 1e-12)
    bvec = kk * a
    hb = lambda t: heads(t.astype(BF16))
    v_h = hb(v)
    at = hb(-kk * jnp.exp(g - lw))
    rt = hb(r * jnp.exp(g))
    btkt = jnp.concatenate([hb(bvec * e_neg), hb(kmod * e_neg)], axis=1)
    bhkh = jnp.concatenate([hb(bvec * e_end), hb(kmod * e_end)], axis=1)
    s0 = s_ref[...]
    s0b = s0.astype(BF16)
    bdot = lambda spec, x, y: jnp.einsum(spec, x, y, preferred_element_type=F32)

    aa = bdot("htk,hsk->hts", jnp.concatenate([at, rt], axis=1), btkt)
    row2 = lax.broadcasted_iota(jnp.int32, (tc, 2 * tc), 0)
    col2 = lax.broadcasted_iota(jnp.int32, (tc, 2 * tc), 1)
    col2 = jnp.where(col2 >= tc, col2 - tc, col2)
    m_a = jnp.where((row2 > col2)[None], aa[:, :tc, :], 0.0).astype(BF16)
    m_r = jnp.where((row2 >= col2)[None], aa[:, tc:, :], 0.0).astype(BF16)

    rhs = bdot("htk,hvk->htv", at, s0b) + bdot(
        "hts,hsv->htv", m_a, jnp.concatenate([jnp.zeros_like(v_h), v_h], axis=1))
    eye = jnp.where(row == col, 1.0, 0.0)[None]
    apow = m_a[:, :, :tc]
    inv = eye + apow.astype(F32)
    span = 2
    while span < tc:
        apow = bdot("hts,hsu->htu", apow, apow).astype(BF16)
        inv = inv + bdot("hts,hsu->htu", apow, inv.astype(BF16))
        span *= 2
    sa = bdot("hts,hsv->htv", inv.astype(BF16), rhs.astype(BF16))

    sv = jnp.concatenate([sa.astype(BF16), v_h], axis=1)
    out = bdot("htk,hvk->htv", rt, s0b) + bdot("hts,hsv->htv", m_r, sv)
    s_ref[...] = s0 * heads(jnp.exp(g_end).reshape(nb, e)) + bdot("htv,htk->hvk", sv, bhkh)

    out = jnp.concatenate(
        [jnp.concatenate([out[b * nh + h] for h in range(nh)], axis=-1) for b in range(nb)], axis=0)
    dev = out - seg_sum(out, 1) * (1.0 / n)
    var = seg_sum(dev * dev, 1) * (1.0 / n)
    bonus = seg_sum(r * kmod * rk_ref[...], 1) * v
    res = dev * lax.rsqrt(var + RWKV_GN_EPS) * lgx_ref[...] + lbx_ref[...] + bonus

    y = _dot((res * _silu(z)).astype(BF16), wout_ref[...])
    pe = _dot(p_ref[0].reshape(rows, p_ref.shape[-1]).astype(BF16), wpe_ref[...])
    o_ref[...] = _tail(x, y, pe, wpg_ref, lng_ref, lnb_ref, alpha).reshape(nb, tc, d)


def _rwkv_layer(x, p, layer, mu, w_rkvz, w0, w1, w2, a0, a1, a2, k_k, k_a, r_k, lnx_g, lnx_b, w_out,
                wpe, wpg, ln_g, ln_b, alpha):
    b, l, d = x.shape
    e = w_out.shape[0]
    nh = e // RWKV_HEAD
    tc = min(RWKV_CHUNK, l)
    nb = RWKV_BATCH if b % RWKV_BATCH == 0 else 1
    row1 = lambda t: t.reshape(1, -1)
    chunk_rows = lambda width: pl.BlockSpec((nb, tc, width), lambda i, t: (i, t, 0))
    return pl.pallas_call(
        functools.partial(_rwkv_layer_kernel, alpha=alpha),
        grid=(b // nb, l // tc),
        in_specs=[chunk_rows(d), pl.BlockSpec((1, nb, tc, p.shape[-1]), lambda i, t: (layer, i, t, 0)),
                  _full(mu.shape), _full(w_rkvz.shape), _full((1, e)), _full(w1.shape), _full(w2.shape),
                  _full((1, e)), _full(a1.shape), _full(a2.shape)] + [_full((1, e))] * 5
                 + [_full(w_out.shape), _full(wpe.shape), _full(wpg.shape), _full((1, d)), _full((1, d))],
        out_specs=chunk_rows(d),
        out_shape=jax.ShapeDtypeStruct((b, l, d), F32),
        scratch_shapes=[pltpu.VMEM((nb * nh, RWKV_HEAD, RWKV_HEAD), F32),
                        pltpu.VMEM((nb, SUBLANES, d), F32)],
        compiler_params=_seq_params(),
        name="rwkv_layer",
    )(x, p, mu, w_rkvz.astype(BF16), row1(w0), w1.astype(BF16), w2.astype(BF16), row1(a0),
      a1.astype(BF16), a2.astype(BF16), row1(k_k), row1(k_a), row1(r_k), row1(lnx_g), row1(lnx_b),
      w_out.astype(BF16), wpe.astype(BF16), wpg.astype(BF16), row1(ln_g), row1(ln_b))


def kernel(x, p, conv_w_in, conv_k, conv_w_out, ssm_w_in, ssm_lam_re, ssm_lam_im, ssm_log_dt, ssm_b_re, ssm_b_im, ssm_c_re, ssm_c_im, ssm_d, ssm_w_glu, ssm_b_glu, ssm_w_out, rwkv_mu, rwkv_w_rkvz, rwkv_w0, rwkv_w1, rwkv_w2, rwkv_a0, rwkv_a1, rwkv_a2, rwkv_k_k, rwkv_k_a, rwkv_r_k, rwkv_lnx_g, rwkv_lnx_b, rwkv_w_out, ple_proj, ple_gate, ln_g, ln_b):
    depth = p.shape[0]
    alpha = (2 * depth) ** 0.25
    for i in range(depth):
        kind, j = i % 3, i // 3
        post = (ple_proj[i], ple_gate[i], ln_g[i], ln_b[i], alpha)
        if kind == 0:
            x = _conv_layer(x, p, i, conv_w_in[j], conv_k[j], conv_w_out[j], *post)
        elif kind == 1:
            x = _s5_layer(x, p, i, ssm_w_in[j], ssm_lam_re[j], ssm_lam_im[j], ssm_log_dt[j],
                          ssm_b_re[j], ssm_b_im[j], ssm_c_re[j], ssm_c_im[j], ssm_d[j],
                          ssm_w_glu[j], ssm_b_glu[j], ssm_w_out[j], *post)
        else:
            x = _rwkv_layer(x, p, i, rwkv_mu[j], rwkv_w_rkvz[j], rwkv_w0[j], rwkv_w1[j], rwkv_w2[j],
                            rwkv_a0[j], rwkv_a1[j], rwkv_a2[j], rwkv_k_k[j], rwkv_k_a[j],
                            rwkv_r_k[j].reshape(-1), rwkv_lnx_g[j], rwkv_lnx_b[j], rwkv_w_out[j], *post)
    return x
```

```python
import functools

import jax
import jax.numpy as jnp
from jax import lax
from jax.experimental import pallas as pl
from jax.experimental.pallas import tpu as pltpu

F32 = jnp.float32
BF16 = jnp.bfloat16

LN_EPS = 1e-5
RWKV_GN_EPS = 64e-5
RWKV_HEAD = 64
SSM_GROUP = 16
SUBLANES = 8
LANES = 128
SLAB_GROUPS = LANES // SSM_GROUP
VMEM_LIMIT = 56 * 1024 * 1024

ROW_TILE = 256
CONV_ROW_TILE = 512
CONV_SUBTILES = 4
RWKV_CHUNK = 64
RWKV_BATCH = 4


def _dot(a, b):
    return jnp.dot(a, b, preferred_element_type=F32)


def _split(t, terms):
    pieces = []
    for i in range(terms):
        pieces.append(t.astype(BF16))
        if i + 1 < terms:
            t = t - pieces[-1].astype(F32)
    return pieces


def _silu(z):
    return z * jax.nn.sigmoid(z)


def _shift_rows(cur, prev, shift):
    rolled = pltpu.roll(cur, shift, 0)
    rows = lax.broadcasted_iota(jnp.int32, (SUBLANES, 1), 0)
    top = jnp.where(rows >= shift, rolled[:SUBLANES], pltpu.roll(prev, shift, 0))
    return jnp.concatenate([top, rolled[SUBLANES:]], axis=0)


def _embed(p_ref, wpe_ref):
    return _dot(p_ref[0, 0].astype(BF16), wpe_ref[...])


def _tail(x, y, pe, wpg_ref, lng_ref, lnb_ref, alpha):
    r = alpha * x + y
    r = r + pe * jax.nn.sigmoid(_dot(r.astype(BF16), wpg_ref[...]))
    d = r - jnp.mean(r, axis=-1, keepdims=True)
    var = jnp.mean(d * d, axis=-1, keepdims=True)
    return d * lax.rsqrt(var + LN_EPS) * lng_ref[...] + lnb_ref[...]


def _conv_layer_kernel(x_ref, p_ref, win_ref, ck_ref, wout_ref, wpe_ref, wpg_ref, lng_ref, lnb_ref,
                       o_ref, carry_ref, *, alpha):
    e = wout_ref.shape[0]
    sub = x_ref.shape[1] // CONV_SUBTILES

    @pl.when(pl.program_id(1) == 0)
    def _():
        carry_ref[...] = jnp.zeros_like(carry_ref)

    def mix(i, prev):
        x = x_ref[0, i * sub:(i + 1) * sub, :]
        xb = x.astype(BF16)
        proj = lambda c: _dot(xb, win_ref[:, c * e:(c + 1) * e])
        u = proj(1) * proj(2)
        conv = (ck_ref[0:1, :] * _shift_rows(u, prev, 2) + ck_ref[1:2, :] * _shift_rows(u, prev, 1)
                + ck_ref[2:3, :] * u)
        return x, proj(0) * conv * _silu(proj(3)), u[sub - SUBLANES:, :]

    def finish(i, x, g):
        y = _dot(g.astype(BF16), wout_ref[...])
        pe = _dot(p_ref[0, 0, i * sub:(i + 1) * sub, :].astype(BF16), wpe_ref[...])
        o_ref[0, i * sub:(i + 1) * sub, :] = _tail(x, y, pe, wpg_ref, lng_ref, lnb_ref, alpha)

    prev, pending = carry_ref[...], None
    for i in range(CONV_SUBTILES):
        x, g, prev = mix(i, prev)
        if pending is not None:
            finish(*pending)
        pending = (i, x, g)
    finish(*pending)
    carry_ref[...] = prev


def _full(shape):
    return pl.BlockSpec(shape, lambda b, t: (0,) * len(shape))


def _rows(tm, width):
    return pl.BlockSpec((1, tm, width), lambda b, t: (b, t, 0))


def _layer_rows(layer, tm, width):
    return pl.BlockSpec((1, 1, tm, width), lambda b, t: (layer, b, t, 0))


def _seq_params():
    return pltpu.CompilerParams(dimension_semantics=("arbitrary", "arbitrary"),
                                vmem_limit_bytes=VMEM_LIMIT)


def _conv_layer(x, p, layer, w_in, conv_k, w_out, wpe, wpg, ln_g, ln_b, alpha):
    b, l, d = x.shape
    e = w_out.shape[0]
    tm = min(CONV_ROW_TILE, l)
    return pl.pallas_call(
        functools.partial(_conv_layer_kernel, alpha=alpha),
        grid=(b, l // tm),
        in_specs=[_rows(tm, d), _layer_rows(layer, tm, p.shape[-1]), _full(w_in.shape), _full(conv_k.shape),
                  _full(w_out.shape), _full(wpe.shape), _full(wpg.shape), _full((1, d)), _full((1, d))],
        out_specs=_rows(tm, d),
        out_shape=jax.ShapeDtypeStruct((b, l, d), F32),
        scratch_shapes=[pltpu.VMEM((SUBLANES, e), F32)],
        compiler_params=_seq_params(),
        name="conv_layer",
    )(x, p, w_in.astype(BF16), conv_k, w_out.astype(BF16), wpe.astype(BF16), wpg.astype(BF16),
      ln_g.reshape(1, d), ln_b.reshape(1, d))


def _s5_discretise_kernel(lre_ref, lim_ref, ldt_ref, bre_ref, bim_ref,
                          are_ref, aim_ref, pre_ref, pim_ref, bbre_ref, bbim_ref, *, seg_len):
    lre, lim = lre_ref[...], lim_ref[...]
    dt = jnp.exp(ldt_ref[...])
    mag = jnp.exp(lre * dt)
    are, aim = mag * jnp.cos(lim * dt), mag * jnp.sin(lim * dt)
    den = lre * lre + lim * lim
    cre = ((are - 1.0) * lre + aim * lim) / den
    cim = (aim * lre - (are - 1.0) * lim) / den
    bre, bim = bre_ref[...], bim_ref[...]
    bbre_ref[...] = cre[:, None, :] * bre - cim[:, None, :] * bim
    bbim_ref[...] = cre[:, None, :] * bim + cim[:, None, :] * bre
    are_ref[...] = are
    aim_ref[...] = aim
    sre, sim = are, aim
    for _ in range(seg_len.bit_length() - 1):
        sre, sim = sre * sre - sim * sim, 2.0 * sre * sim
    pr, pi = sre, sim
    for n in range(SUBLANES):
        pre_ref[n] = pr
        pim_ref[n] = pi
        pr, pi = pr * sre - pi * sim, pr * sim + pi * sre


def _s5_layer_kernel(x_ref, p_ref, win_ref, wbre_ref, wbim_ref, wcre_ref, wcim_ref, tab_ref, dsk_ref,
                     wglu_ref, bglu_ref, wout_ref, wpe_ref, wpg_ref, lng_ref, lnb_ref,
                     o_ref, state_ref, sre_ref, sim_ref, y_ref, *, alpha):
    e = wout_ref.shape[0]
    tm = x_ref.shape[1]
    nslab = wbre_ref.shape[0]
    w = wbre_ref.shape[2]
    seg_len = tm // SUBLANES

    @pl.when(pl.program_id(1) == 0)
    def _():
        state_ref[...] = jnp.zeros_like(state_ref)

    x = x_ref[0]
    xb = x.astype(BF16)
    u = _dot(xb, win_ref[:, :e])

    new_r = lax.broadcasted_iota(jnp.int32, (tm, tm), 0)
    old_r = lax.broadcasted_iota(jnp.int32, (tm, tm), 1)
    regroup = jnp.where(old_r == (new_r % SUBLANES) * seg_len + new_r // SUBLANES, 1.0, 0.0).astype(BF16)
    ungroup = jnp.where(new_r == (old_r % SUBLANES) * seg_len + old_r // SUBLANES, 1.0, 0.0).astype(BF16)
    ub = _dot(regroup, u.astype(BF16)).astype(BF16)
    rows = lax.broadcasted_iota(jnp.int32, (SUBLANES, 1), 0)

    def project_in(s):
        us = ub[:, s * LANES:(s + 1) * LANES]
        sre_ref[s] = _dot(us, wbre_ref[s])
        sim_ref[s] = _dot(us, wbim_ref[s])

    z_parts = []
    z_w = 2 * e // nslab
    project_in(0)
    for s in range(nslab):
        if s + 1 < nslab:
            project_in(s + 1)
        if s % 2 == 0:
            z_parts.append(_dot(xb, win_ref[:, e + (s // 2) * z_w:e + (s // 2 + 1) * z_w]))
        if s == 1:
            pe = _embed(p_ref, wpe_ref)
        sre, sim = sre_ref.at[s], sim_ref.at[s]
        ar, ai = tab_ref[s, 0], tab_ref[s, 1]

        def local_final(i, c):
            cr, ci = c
            r0 = pl.multiple_of(i * SUBLANES, SUBLANES)
            return (cr * ar - ci * ai + sre[pl.ds(r0, SUBLANES), :],
                    cr * ai + ci * ar + sim[pl.ds(r0, SUBLANES), :])

        zero = jnp.zeros((SUBLANES, w), F32)
        fr, fi = lax.fori_loop(0, seg_len, local_final, (zero, zero), unroll=True)
        for lvl, dist in enumerate((1, 2, 4)):
            pr, pi = tab_ref[s, 2 + 2 * lvl], tab_ref[s, 3 + 2 * lvl]
            gr, gi = pltpu.roll(fr, dist, 0), pltpu.roll(fi, dist, 0)
            fr, fi = fr + pr * gr - pi * gi, fi + pr * gi + pi * gr
        cr0, ci0 = state_ref[2 * s:2 * s + 1, :], state_ref[2 * s + 1:2 * s + 2, :]
        qr, qi = tab_ref[s, 8], tab_ref[s, 9]
        init_r = qr * cr0 - qi * ci0 + jnp.where(rows >= 1, pltpu.roll(fr, 1, 0), 0.0)
        init_i = qr * ci0 + qi * cr0 + jnp.where(rows >= 1, pltpu.roll(fi, 1, 0), 0.0)

        def all_states(i, c):
            cr, ci = c
            r0 = pl.multiple_of(i * SUBLANES, SUBLANES)
            nr = cr * ar - ci * ai + sre[pl.ds(r0, SUBLANES), :]
            ni = cr * ai + ci * ar + sim[pl.ds(r0, SUBLANES), :]
            sre[pl.ds(r0, SUBLANES), :] = nr
            sim[pl.ds(r0, SUBLANES), :] = ni
            return nr, ni

        lr, li = lax.fori_loop(0, seg_len, all_states, (init_r, init_i), unroll=True)
        state_ref[2 * s:2 * s + 1, :] = lr[SUBLANES - 1:, :]
        state_ref[2 * s + 1:2 * s + 2, :] = li[SUBLANES - 1:, :]
        y_ref[:, s * LANES:(s + 1) * LANES] = (_dot(sre[...].astype(BF16), wcre_ref[s])
                                               + _dot(sim[...].astype(BF16), wcim_ref[s]))

    y = sum(_dot(ungroup, piece) for piece in _split(y_ref[...], 2))

    y = jax.nn.gelu(y + dsk_ref[...] * u)
    y = y * jax.nn.sigmoid(_dot(y.astype(BF16), wglu_ref[...]) + bglu_ref[...])
    z = jnp.concatenate(z_parts, axis=-1)
    out = _dot((y * _silu(z)).astype(BF16), wout_ref[...])
    o_ref[0] = _tail(x, out, pe, wpg_ref, lng_ref, lnb_ref, alpha)


def _s5_layer(x, p, layer, w_in, lam_re, lam_im, log_dt, b_re, b_im, c_re, c_im, d_skip, w_glu, b_glu, w_out,
              wpe, wpg, ln_g, ln_b, alpha):
    b, l, d = x.shape
    e = w_out.shape[0]
    g, pst = lam_re.shape
    hch = b_re.shape[-1]
    nslab = g // SLAB_GROUPS
    w = SLAB_GROUPS * pst
    tm = min(ROW_TILE, l)
    seg_len = tm // SUBLANES
    assert seg_len & (seg_len - 1) == 0, "segment length must be a power of two"

    gp1 = jax.ShapeDtypeStruct((g, pst), F32)
    gp = jax.ShapeDtypeStruct((SUBLANES, g, pst), F32)
    gb = jax.ShapeDtypeStruct((g, hch, pst), F32)
    a_re, a_im, pw_re, pw_im, bb_re, bb_im = pl.pallas_call(
        functools.partial(_s5_discretise_kernel, seg_len=seg_len),
        out_shape=(gp1, gp1, gp, gp, gb, gb), name="s5_discretise",
    )(lam_re, lam_im, log_dt.reshape(g, 1), jnp.swapaxes(b_re, 1, 2), jnp.swapaxes(b_im, 1, 2))

    eye = jnp.eye(SLAB_GROUPS, dtype=F32)
    blk_in = lambda m: jnp.einsum("sghp,gk->sghkp", m.reshape(nslab, SLAB_GROUPS, hch, pst),
                                  eye).reshape(nslab, LANES, w).astype(BF16)
    blk_out = lambda m: jnp.einsum("sghp,gk->skpgh", m.reshape(nslab, SLAB_GROUPS, hch, pst),
                                   eye).reshape(nslab, w, LANES).astype(BF16)
    rows = jnp.arange(SUBLANES)[:, None, None]
    flat = lambda m: m.reshape(SUBLANES, nslab, w)
    tabs = [jnp.broadcast_to(m.reshape(1, nslab, w), (SUBLANES, nslab, w)) for m in (a_re, a_im)]
    for dist in (1, 2, 4):
        tabs += [jnp.where(rows >= dist, flat(m)[dist - 1][None], 0.0) for m in (pw_re, pw_im)]
    tabs.append(jnp.concatenate([jnp.ones((1, nslab, w), F32), flat(pw_re)[:SUBLANES - 1]]))
    tabs.append(jnp.concatenate([jnp.zeros((1, nslab, w), F32), flat(pw_im)[:SUBLANES - 1]]))
    tab = jnp.transpose(jnp.stack(tabs), (2, 0, 1, 3))

    return pl.pallas_call(
        functools.partial(_s5_layer_kernel, alpha=alpha),
        grid=(b, l // tm),
        in_specs=[_rows(tm, d), _layer_rows(layer, tm, p.shape[-1]), _full(w_in.shape),
                  _full((nslab, LANES, w)), _full((nslab, LANES, w)),
                  _full((nslab, w, LANES)), _full((nslab, w, LANES)),
                  _full(tab.shape), _full((1, e)),
                  _full(w_glu.shape), _full((1, e)), _full(w_out.shape), _full(wpe.shape),
                  _full(wpg.shape), _full((1, d)), _full((1, d))],
        out_specs=_rows(tm, d),
        out_shape=jax.ShapeDtypeStruct((b, l, d), F32),
        scratch_shapes=[pltpu.VMEM((2 * nslab, w), F32), pltpu.VMEM((nslab, tm, w), F32),
                        pltpu.VMEM((nslab, tm, w), F32), pltpu.VMEM((tm, e), F32)],
        compiler_params=_seq_params(),
        name="s5_layer",
    )(x, p, w_in.astype(BF16), blk_in(bb_re), blk_in(bb_im),
      blk_out(c_re), blk_out(-c_im),
      tab, d_skip.reshape(1, e), w_glu.astype(BF16), b_glu.reshape(1, e), w_out.astype(BF16),
      wpe.astype(BF16), wpg.astype(BF16), ln_g.reshape(1, d), ln_b.reshape(1, d))


def _rwkv_layer_kernel(x_ref, p_ref, mu_ref, wrkvz_ref, w0_ref, w1_ref, w2_ref, a0_ref, a1_ref, a2_ref,
                       kk_ref, ka_ref, rk_ref, lgx_ref, lbx_ref, wout_ref, wpe_ref, wpg_ref, lng_ref, lnb_ref,
                       o_ref, s_ref, carry_ref, *, alpha):
    nb, tc, d = x_ref.shape
    e = wout_ref.shape[0]
    n = s_ref.shape[1]
    nh = s_ref.shape[0] // nb
    rows = nb * tc

    @pl.when(pl.program_id(1) == 0)
    def _():
        s_ref[...] = jnp.zeros_like(s_ref)
        carry_ref[...] = jnp.zeros_like(carry_ref)

    x3 = x_ref[...]
    x = x3.reshape(rows, d)
    shifted = jnp.concatenate([_shift_rows(x3[b], carry_ref[b], 1) for b in range(nb)], axis=0)
    carry_ref[...] = x3[:, tc - SUBLANES:, :]
    dx = shifted - x
    mix = lambda i: (x + dx * mu_ref[i:i + 1, :]).astype(BF16)
    r, k, v, z = (_dot(mix(i), wrkvz_ref[i]) for i in range(4))
    lora_w = _dot(jnp.tanh(_dot(mix(4), w1_ref[...])).astype(BF16), w2_ref[...])
    w_log = -jax.nn.softplus(-(w0_ref[...] + lora_w)) - 0.5
    lw = -jnp.exp(w_log)
    lora_a = _dot(_dot(mix(5), a1_ref[...]).astype(BF16), a2_ref[...])
    a = jax.nn.sigmoid(a0_ref[...] + lora_a)

    heads = lambda t: jnp.stack([t[(t.shape[0] // nb) * b:(t.shape[0] // nb) * (b + 1), n * h:n * (h + 1)]
                                 for b in range(nb) for h in range(nh)])

    brow = lax.broadcasted_iota(jnp.int32, (rows, rows), 0)
    bcol = lax.broadcasted_iota(jnp.int32, (rows, rows), 1)
    tri = jnp.where((brow >= bcol) & (brow // tc == bcol // tc), 1.0, 0.0).astype(BF16)
    g = sum(_dot(tri, piece) for piece in _split(lw, 2))
    g_end = g.reshape(nb, tc, e)[:, tc - 1:, :]

    seg_w = 2 * LANES
    srow = lax.broadcasted_iota(jnp.int32, (seg_w, seg_w), 0)
    scol = lax.broadcasted_iota(jnp.int32, (seg_w, seg_w), 1)
    seg_ones = jnp.where(srow // n == scol // n, 1.0, 0.0).astype(BF16)

    def seg_sum(t, terms):
        pieces = _split(t, terms)
        return jnp.concatenate([sum(_dot(piece[:, j:j + seg_w], seg_ones) for piece in pieces)
                                for j in range(0, e, seg_w)], axis=-1)

    kmod = k * (1.0 + (a - 1.0) * ka_ref[...])
    e_neg = jnp.exp(-g)
    e_end = jnp.exp(g_end - g.reshape(nb, tc, e)).reshape(rows, e)
    kkr = k * kk_ref[...]
    kk = kkr / jnp.maximum(jnp.sqrt(seg_sum(kkr * kkr, 2)), 1e-12)
    bvec = kk * a
    hb = lambda t: heads(t.astype(BF16))
    v_h = hb(v)
    at = hb(-kk * jnp.exp(g - lw))
    rt = hb(r * jnp.exp(g))
    btkt = jnp.concatenate([hb(bvec * e_neg), hb(kmod * e_neg)], axis=1)
    bhkh = jnp.concatenate([hb(bvec * e_end), hb(kmod * e_end)], axis=1)
    s0 = s_ref[...]
    s0b = s0.astype(BF16)
    bdot = lambda spec, x, y: jnp.einsum(spec, x, y, preferred_element_type=F32)

    aa = bdot("htk,hsk->hts", jnp.concatenate([at, rt], axis=1), btkt)
    row2 = lax.broadcasted_iota(jnp.int32, (tc, 2 * tc), 0)
    lane2 = lax.broadcasted_iota(jnp.int32, (tc, 2 * tc), 1)
    right = (lane2 >= tc)[None]
    col2 = jnp.where(lane2 >= tc, lane2 - tc, lane2)
    m_a = jnp.where((row2 > col2)[None], aa[:, :tc, :], 0.0).astype(BF16)
    m_r = jnp.where((row2 >= col2)[None], aa[:, tc:, :], 0.0).astype(BF16)

    rhs = bdot("htk,hvk->htv", at, s0b) + bdot(
        "hts,hsv->htv", m_a, jnp.concatenate([jnp.zeros_like(v_h), v_h], axis=1))
    pair = jnp.where(right, jnp.where(row2 == col2, 1.0, 0.0)[None], m_a.astype(F32))
    span = 1
    while span < tc:
        pair_b = pair.astype(BF16)
        pair = bdot("hts,hsu->htu", pair_b[:, :, :tc], pair_b) + jnp.where(right, pair, 0.0)
        span *= 2
    rhs_b = rhs.astype(BF16)
    sa = bdot("hts,hsv->htv", pair.astype(BF16), jnp.concatenate([jnp.zeros_like(rhs_b), rhs_b], axis=1))

    sv = jnp.concatenate([sa.astype(BF16), v_h], axis=1)
    out = bdot("htk,hvk->htv", rt, s0b) + bdot("hts,hsv->htv", m_r, sv)
    s_ref[...] = s0 * heads(jnp.exp(g_end).reshape(nb, e)) + bdot("htv,htk->hvk", sv, bhkh)

    out = jnp.concatenate(
        [jnp.concatenate([out[b * nh + h] for h in range(nh)], axis=-1) for b in range(nb)], axis=0)
    dev = out - seg_sum(out, 1) * (1.0 / n)
    var = seg_sum(dev * dev, 1) * (1.0 / n)
    bonus = seg_sum(r * kmod * rk_ref[...], 1) * v
    res = dev * lax.rsqrt(var + RWKV_GN_EPS) * lgx_ref[...] + lbx_ref[...] + bonus

    y = _dot((res * _silu(z)).astype(BF16), wout_ref[...])
    pe = _dot(p_ref[0].reshape(rows, p_ref.shape[-1]).astype(BF16), wpe_ref[...])
    o_ref[...] = _tail(x, y, pe, wpg_ref, lng_ref, lnb_ref, alpha).reshape(nb, tc, d)


def _rwkv_layer(x, p, layer, mu, w_rkvz, w0, w1, w2, a0, a1, a2, k_k, k_a, r_k, lnx_g, lnx_b, w_out,
                wpe, wpg, ln_g, ln_b, alpha):
    b, l, d = x.shape
    e = w_out.shape[0]
    nh = e // RWKV_HEAD
    tc = min(RWKV_CHUNK, l)
    nb = RWKV_BATCH if b % RWKV_BATCH == 0 else 1
    row1 = lambda t: t.reshape(1, -1)
    chunk_rows = lambda width: pl.BlockSpec((nb, tc, width), lambda i, t: (i, t, 0))
    return pl.pallas_call(
        functools.partial(_rwkv_layer_kernel, alpha=alpha),
        grid=(b // nb, l // tc),
        in_specs=[chunk_rows(d), pl.BlockSpec((1, nb, tc, p.shape[-1]), lambda i, t: (layer, i, t, 0)),
                  _full(mu.shape), _full(w_rkvz.shape), _full((1, e)), _full(w1.shape), _full(w2.shape),
                  _full((1, e)), _full(a1.shape), _full(a2.shape)] + [_full((1, e))] * 5
                 + [_full(w_out.shape), _full(wpe.shape), _full(wpg.shape), _full((1, d)), _full((1, d))],
        out_specs=chunk_rows(d),
        out_shape=jax.ShapeDtypeStruct((b, l, d), F32),
        scratch_shapes=[pltpu.VMEM((nb * nh, RWKV_HEAD, RWKV_HEAD), F32),
                        pltpu.VMEM((nb, SUBLANES, d), F32)],
        compiler_params=_seq_params(),
        name="rwkv_layer",
    )(x, p, mu, w_rkvz.astype(BF16), row1(w0), w1.astype(BF16), w2.astype(BF16), row1(a0),
      a1.astype(BF16), a2.astype(BF16), row1(k_k), row1(k_a), row1(r_k), row1(lnx_g), row1(lnx_b),
      w_out.astype(BF16), wpe.astype(BF16), wpg.astype(BF16), row1(ln_g), row1(ln_b))


def kernel(x, p, conv_w_in, conv_k, conv_w_out, ssm_w_in, ssm_lam_re, ssm_lam_im, ssm_log_dt, ssm_b_re, ssm_b_im, ssm_c_re, ssm_c_im, ssm_d, ssm_w_glu, ssm_b_glu, ssm_w_out, rwkv_mu, rwkv_w_rkvz, rwkv_w0, rwkv_w1, rwkv_w2, rwkv_a0, rwkv_a1, rwkv_a2, rwkv_k_k, rwkv_k_a, rwkv_r_k, rwkv_lnx_g, rwkv_lnx_b, rwkv_w_out, ple_proj, ple_gate, ln_g, ln_b):
    depth = p.shape[0]
    alpha = (2 * depth) ** 0.25
    for i in range(depth):
        kind, j = i % 3, i // 3
        post = (ple_proj[i], ple_gate[i], ln_g[i], ln_b[i], alpha)
        if kind == 0:
            x = _conv_layer(x, p, i, conv_w_in[j], conv_k[j], conv_w_out[j], *post)
        elif kind == 1:
            x = _s5_layer(x, p, i, ssm_w_in[j], ssm_lam_re[j], ssm_lam_im[j], ssm_log_dt[j],
                          ssm_b_re[j], ssm_b_im[j], ssm_c_re[j], ssm_c_im[j], ssm_d[j],
                          ssm_w_glu[j], ssm_b_glu[j], ssm_w_out[j], *post)
        else:
            x = _rwkv_layer(x, p, i, rwkv_mu[j], rwkv_w_rkvz[j], rwkv_w0[j], rwkv_w1[j], rwkv_w2[j],
                            rwkv_a0[j], rwkv_a1[j], rwkv_a2[j], rwkv_k_k[j], rwkv_k_a[j],
                            rwkv_r_k[j].reshape(-1), rwkv_lnx_g[j], rwkv_lnx_b[j], rwkv_w_out[j], *post)
    return x
```

```python
import functools

import jax
import jax.numpy as jnp
from jax import lax
from jax.experimental import pallas as pl
from jax.experimental.pallas import tpu as pltpu

F32 = jnp.float32
BF16 = jnp.bfloat16

LN_EPS = 1e-5
RWKV_GN_EPS = 64e-5
RWKV_HEAD = 64
SSM_GROUP = 16
SUBLANES = 8
LANES = 128
SLAB_GROUPS = LANES // SSM_GROUP
VMEM_LIMIT = 56 * 1024 * 1024

ROW_TILE = 256
CONV_ROW_TILE = 512
CONV_SUBTILES = 4
RWKV_CHUNK = 64
RWKV_BATCH = 4


def _dot(a, b):
    return jnp.dot(a, b, preferred_element_type=F32)


def _split(t, terms):
    pieces = []
    for i in range(terms):
        pieces.append(t.astype(BF16))
        if i + 1 < terms:
            t = t - pieces[-1].astype(F32)
    return pieces


def _silu(z):
    return z * jax.nn.sigmoid(z)


def _shift_rows(cur, prev, shift):
    rolled = pltpu.roll(cur, shift, 0)
    rows = lax.broadcasted_iota(jnp.int32, (SUBLANES, 1), 0)
    top = jnp.where(rows >= shift, rolled[:SUBLANES], pltpu.roll(prev, shift, 0))
    return jnp.concatenate([top, rolled[SUBLANES:]], axis=0)


def _embed(p_ref, wpe_ref):
    return _dot(p_ref[0, 0].astype(BF16), wpe_ref[...])


def _tail(x, y, pe, wpg_ref, lng_ref, lnb_ref, alpha):
    r = alpha * x + y
    r = r + pe * jax.nn.sigmoid(_dot(r.astype(BF16), wpg_ref[...]))
    d = r - jnp.mean(r, axis=-1, keepdims=True)
    var = jnp.mean(d * d, axis=-1, keepdims=True)
    return d * lax.rsqrt(var + LN_EPS) * lng_ref[...] + lnb_ref[...]


def _conv_layer_kernel(x_ref, p_ref, win_ref, ck_ref, wout_ref, wpe_ref, wpg_ref, lng_ref, lnb_ref,
                       o_ref, carry_ref, *, alpha):
    e = wout_ref.shape[0]
    sub = x_ref.shape[1] // CONV_SUBTILES

    @pl.when(pl.program_id(1) == 0)
    def _():
        carry_ref[...] = jnp.zeros_like(carry_ref)

    def mix(i, prev):
        x = x_ref[0, i * sub:(i + 1) * sub, :]
        xb = x.astype(BF16)
        proj = lambda c: _dot(xb, win_ref[:, c * e:(c + 1) * e])
        u = proj(1) * proj(2)
        conv = (ck_ref[0:1, :] * _shift_rows(u, prev, 2) + ck_ref[1:2, :] * _shift_rows(u, prev, 1)
                + ck_ref[2:3, :] * u)
        return x, proj(0) * conv * _silu(proj(3)), u[sub - SUBLANES:, :]

    def finish(i, x, g):
        y = _dot(g.astype(BF16), wout_ref[...])
        pe = _dot(p_ref[0, 0, i * sub:(i + 1) * sub, :].astype(BF16), wpe_ref[...])
        o_ref[0, i * sub:(i + 1) * sub, :] = _tail(x, y, pe, wpg_ref, lng_ref, lnb_ref, alpha)

    prev, pending = carry_ref[...], None
    for i in range(CONV_SUBTILES):
        x, g, prev = mix(i, prev)
        if pending is not None:
            finish(*pending)
        pending = (i, x, g)
    finish(*pending)
    carry_ref[...] = prev


def _full(shape):
    return pl.BlockSpec(shape, lambda b, t: (0,) * len(shape))


def _rows(tm, width):
    return pl.BlockSpec((1, tm, width), lambda b, t: (b, t, 0))


def _layer_rows(layer, tm, width):
    return pl.BlockSpec((1, 1, tm, width), lambda b, t: (layer, b, t, 0))


def _seq_params():
    return pltpu.CompilerParams(dimension_semantics=("arbitrary", "arbitrary"),
                                vmem_limit_bytes=VMEM_LIMIT)


def _conv_layer(x, p, layer, w_in, conv_k, w_out, wpe, wpg, ln_g, ln_b, alpha):
    b, l, d = x.shape
    e = w_out.shape[0]
    tm = min(CONV_ROW_TILE, l)
    return pl.pallas_call(
        functools.partial(_conv_layer_kernel, alpha=alpha),
        grid=(b, l // tm),
        in_specs=[_rows(tm, d), _layer_rows(layer, tm, p.shape[-1]), _full(w_in.shape), _full(conv_k.shape),
                  _full(w_out.shape), _full(wpe.shape), _full(wpg.shape), _full((1, d)), _full((1, d))],
        out_specs=_rows(tm, d),
        out_shape=jax.ShapeDtypeStruct((b, l, d), F32),
        scratch_shapes=[pltpu.VMEM((SUBLANES, e), F32)],
        compiler_params=_seq_params(),
        name="conv_layer",
    )(x, p, w_in.astype(BF16), conv_k, w_out.astype(BF16), wpe.astype(BF16), wpg.astype(BF16),
      ln_g.reshape(1, d), ln_b.reshape(1, d))


def _s5_discretise_kernel(lre_ref, lim_ref, ldt_ref, bre_ref, bim_ref,
                          are_ref, aim_ref, pre_ref, pim_ref, bbre_ref, bbim_ref, *, seg_len):
    lre, lim = lre_ref[...], lim_ref[...]
    dt = jnp.exp(ldt_ref[...])
    mag = jnp.exp(lre * dt)
    are, aim = mag * jnp.cos(lim * dt), mag * jnp.sin(lim * dt)
    den = lre * lre + lim * lim
    cre = ((are - 1.0) * lre + aim * lim) / den
    cim = (aim * lre - (are - 1.0) * lim) / den
    bre, bim = bre_ref[...], bim_ref[...]
    bbre_ref[...] = cre[:, None, :] * bre - cim[:, None, :] * bim
    bbim_ref[...] = cre[:, None, :] * bim + cim[:, None, :] * bre
    are_ref[...] = are
    aim_ref[...] = aim
    sre, sim = are, aim
    for _ in range(seg_len.bit_length() - 1):
        sre, sim = sre * sre - sim * sim, 2.0 * sre * sim
    pr, pi = sre, sim
    for n in range(SUBLANES):
        pre_ref[n] = pr
        pim_ref[n] = pi
        pr, pi = pr * sre - pi * sim, pr * sim + pi * sre


def _s5_layer_kernel(x_ref, p_ref, win_ref, wbre_ref, wbim_ref, wcre_ref, wcim_ref, tab_ref, dsk_ref,
                     wglu_ref, bglu_ref, wout_ref, wpe_ref, wpg_ref, lng_ref, lnb_ref,
                     o_ref, state_ref, sre_ref, sim_ref, y_ref, *, alpha):
    e = wout_ref.shape[0]
    tm = x_ref.shape[1]
    nslab = wbre_ref.shape[0]
    w = wbre_ref.shape[2]
    seg_len = tm // SUBLANES

    @pl.when(pl.program_id(1) == 0)
    def _():
        state_ref[...] = jnp.zeros_like(state_ref)

    x = x_ref[0]
    xb = x.astype(BF16)
    u = _dot(xb, win_ref[:, :e])

    new_r = lax.broadcasted_iota(jnp.int32, (tm, tm), 0)
    old_r = lax.broadcasted_iota(jnp.int32, (tm, tm), 1)
    regroup = jnp.where(old_r == (new_r % SUBLANES) * seg_len + new_r // SUBLANES, 1.0, 0.0).astype(BF16)
    ungroup = jnp.where(new_r == (old_r % SUBLANES) * seg_len + old_r // SUBLANES, 1.0, 0.0).astype(BF16)
    ub = _dot(regroup, u.astype(BF16)).astype(BF16)
    rows = lax.broadcasted_iota(jnp.int32, (SUBLANES, 1), 0)

    def project_in(s):
        us = ub[:, s * LANES:(s + 1) * LANES]
        sre_ref[s] = _dot(us, wbre_ref[s])
        sim_ref[s] = _dot(us, wbim_ref[s])

    z_parts = []
    z_w = 2 * e // nslab
    project_in(0)
    for s in range(nslab):
        if s + 1 < nslab:
            project_in(s + 1)
        if s % 2 == 0:
            z_parts.append(_dot(xb, win_ref[:, e + (s // 2) * z_w:e + (s // 2 + 1) * z_w]))
        if s == 1:
            pe = _embed(p_ref, wpe_ref)
        sre, sim = sre_ref.at[s], sim_ref.at[s]
        ar, ai = tab_ref[s, 0], tab_ref[s, 1]

        def local_final(i, c):
            cr, ci = c
            r0 = pl.multiple_of(i * SUBLANES, SUBLANES)
            return (cr * ar - ci * ai + sre[pl.ds(r0, SUBLANES), :],
                    cr * ai + ci * ar + sim[pl.ds(r0, SUBLANES), :])

        zero = jnp.zeros((SUBLANES, w), F32)
        fr, fi = lax.fori_loop(0, seg_len, local_final, (zero, zero), unroll=True)
        for lvl, dist in enumerate((1, 2, 4)):
            pr, pi = tab_ref[s, 2 + 2 * lvl], tab_ref[s, 3 + 2 * lvl]
            gr, gi = pltpu.roll(fr, dist, 0), pltpu.roll(fi, dist, 0)
            fr, fi = fr + pr * gr - pi * gi, fi + pr * gi + pi * gr
        cr0, ci0 = state_ref[2 * s:2 * s + 1, :], state_ref[2 * s + 1:2 * s + 2, :]
        qr, qi = tab_ref[s, 8], tab_ref[s, 9]
        init_r = qr * cr0 - qi * ci0 + jnp.where(rows >= 1, pltpu.roll(fr, 1, 0), 0.0)
        init_i = qr * ci0 + qi * cr0 + jnp.where(rows >= 1, pltpu.roll(fi, 1, 0), 0.0)

        def all_states(i, c):
            cr, ci = c
            r0 = pl.multiple_of(i * SUBLANES, SUBLANES)
            nr = cr * ar - ci * ai + sre[pl.ds(r0, SUBLANES), :]
            ni = cr * ai + ci * ar + sim[pl.ds(r0, SUBLANES), :]
            sre[pl.ds(r0, SUBLANES), :] = nr
            sim[pl.ds(r0, SUBLANES), :] = ni
            return nr, ni

        lr, li = lax.fori_loop(0, seg_len, all_states, (init_r, init_i), unroll=True)
        state_ref[2 * s:2 * s + 1, :] = lr[SUBLANES - 1:, :]
        state_ref[2 * s + 1:2 * s + 2, :] = li[SUBLANES - 1:, :]
        y_ref[:, s * LANES:(s + 1) * LANES] = (_dot(sre[...].astype(BF16), wcre_ref[s])
                                               + _dot(sim[...].astype(BF16), wcim_ref[s]))

    y = sum(_dot(ungroup, piece) for piece in _split(y_ref[...], 2))

    y = jax.nn.gelu(y + dsk_ref[...] * u)
    y = y * jax.nn.sigmoid(_dot(y.astype(BF16), wglu_ref[...]) + bglu_ref[...])
    z = jnp.concatenate(z_parts, axis=-1)
    out = _dot((y * _silu(z)).astype(BF16), wout_ref[...])
    o_ref[0] = _tail(x, out, pe, wpg_ref, lng_ref, lnb_ref, alpha)


def _s5_layer(x, p, layer, w_in, lam_re, lam_im, log_dt, b_re, b_im, c_re, c_im, d_skip, w_glu, b_glu, w_out,
              wpe, wpg, ln_g, ln_b, alpha):
    b, l, d = x.shape
    e = w_out.shape[0]
    g, pst = lam_re.shape
    hch = b_re.shape[-1]
    nslab = g // SLAB_GROUPS
    w = SLAB_GROUPS * pst
    tm = min(ROW_TILE, l)
    seg_len = tm // SUBLANES
    assert seg_len & (seg_len - 1) == 0, "segment length must be a power of two"

    gp1 = jax.ShapeDtypeStruct((g, pst), F32)
    gp = jax.ShapeDtypeStruct((SUBLANES, g, pst), F32)
    gb = jax.ShapeDtypeStruct((g, hch, pst), F32)
    a_re, a_im, pw_re, pw_im, bb_re, bb_im = pl.pallas_call(
        functools.partial(_s5_discretise_kernel, seg_len=seg_len),
        out_shape=(gp1, gp1, gp, gp, gb, gb), name="s5_discretise",
    )(lam_re, lam_im, log_dt.reshape(g, 1), jnp.swapaxes(b_re, 1, 2), jnp.swapaxes(b_im, 1, 2))

    eye = jnp.eye(SLAB_GROUPS, dtype=F32)
    blk_in = lambda m: jnp.einsum("sghp,gk->sghkp", m.reshape(nslab, SLAB_GROUPS, hch, pst),
                                  eye).reshape(nslab, LANES, w).astype(BF16)
    blk_out = lambda m: jnp.einsum("sghp,gk->skpgh", m.reshape(nslab, SLAB_GROUPS, hch, pst),
                                   eye).reshape(nslab, w, LANES).astype(BF16)
    rows = jnp.arange(SUBLANES)[:, None, None]
    flat = lambda m: m.reshape(SUBLANES, nslab, w)
    tabs = [jnp.broadcast_to(m.reshape(1, nslab, w), (SUBLANES, nslab, w)) for m in (a_re, a_im)]
    for dist in (1, 2, 4):
        tabs += [jnp.where(rows >= dist, flat(m)[dist - 1][None], 0.0) for m in (pw_re, pw_im)]
    tabs.append(jnp.concatenate([jnp.ones((1, nslab, w), F32), flat(pw_re)[:SUBLANES - 1]]))
    tabs.append(jnp.concatenate([jnp.zeros((1, nslab, w), F32), flat(pw_im)[:SUBLANES - 1]]))
    tab = jnp.transpose(jnp.stack(tabs), (2, 0, 1, 3))

    return pl.pallas_call(
        functools.partial(_s5_layer_kernel, alpha=alpha),
        grid=(b, l // tm),
        in_specs=[_rows(tm, d), _layer_rows(layer, tm, p.shape[-1]), _full(w_in.shape),
                  _full((nslab, LANES, w)), _full((nslab, LANES, w)),
                  _full((nslab, w, LANES)), _full((nslab, w, LANES)),
                  _full(tab.shape), _full((1, e)),
                  _full(w_glu.shape), _full((1, e)), _full(w_out.shape), _full(wpe.shape),
                  _full(wpg.shape), _full((1, d)), _full((1, d))],
        out_specs=_rows(tm, d),
        out_shape=jax.ShapeDtypeStruct((b, l, d), F32),
        scratch_shapes=[pltpu.VMEM((2 * nslab, w), F32), pltpu.VMEM((nslab, tm, w), F32),
                        pltpu.VMEM((nslab, tm, w), F32), pltpu.VMEM((tm, e), F32)],
        compiler_params=_seq_params(),
        name="s5_layer",
    )(x, p, w_in.astype(BF16), blk_in(bb_re), blk_in(bb_im),
      blk_out(c_re), blk_out(-c_im),
      tab, d_skip.reshape(1, e), w_glu.astype(BF16), b_glu.reshape(1, e), w_out.astype(BF16),
      wpe.astype(BF16), wpg.astype(BF16), ln_g.reshape(1, d), ln_b.reshape(1, d))


def _rwkv_layer_kernel(x_ref, p_ref, mu_ref, wrkvz_ref, w0_ref, w1_ref, w2_ref, a0_ref, a1_ref, a2_ref,
                       kk_ref, ka_ref, rk_ref, lgx_ref, lbx_ref, wout_ref, wpe_ref, wpg_ref, lng_ref, lnb_ref,
                       o_ref, s_ref, carry_ref, *, alpha):
    nb, tc, d = x_ref.shape
    e = wout_ref.shape[0]
    n = s_ref.shape[1]
    nh = s_ref.shape[0] // nb
    rows = nb * tc

    @pl.when(pl.program_id(1) == 0)
    def _():
        s_ref[...] = jnp.zeros_like(s_ref)
        carry_ref[...] = jnp.zeros_like(carry_ref)

    x3 = x_ref[...]
    x = x3.reshape(rows, d)
    shifted = jnp.concatenate([_shift_rows(x3[b], carry_ref[b], 1) for b in range(nb)], axis=0)
    carry_ref[...] = x3[:, tc - SUBLANES:, :]
    dx = shifted - x
    mix = lambda i: (x + dx * mu_ref[i:i + 1, :]).astype(BF16)
    r, k, v, z = (_dot(mix(i), wrkvz_ref[i]) for i in range(4))
    lora_w = _dot(jnp.tanh(_dot(mix(4), w1_ref[...])).astype(BF16), w2_ref[...])
    w_log = -jax.nn.softplus(-(w0_ref[...] + lora_w)) - 0.5
    lw = -jnp.exp(w_log)
    lora_a = _dot(_dot(mix(5), a1_ref[...]).astype(BF16), a2_ref[...])
    a = jax.nn.sigmoid(a0_ref[...] + lora_a)

    heads = lambda t: jnp.stack([t[(t.shape[0] // nb) * b:(t.shape[0] // nb) * (b + 1), n * h:n * (h + 1)]
                                 for b in range(nb) for h in range(nh)])

    brow = lax.broadcasted_iota(jnp.int32, (rows, rows), 0)
    bcol = lax.broadcasted_iota(jnp.int32, (rows, rows), 1)
    tri = jnp.where((brow >= bcol) & (brow // tc == bcol // tc), 1.0, 0.0).astype(BF16)
    g = sum(_dot(tri, piece) for piece in _split(lw, 2))
    g_end = g.reshape(nb, tc, e)[:, tc - 1:, :]

    seg_w = 2 * LANES
    srow = lax.broadcasted_iota(jnp.int32, (seg_w, seg_w), 0)
    scol = lax.broadcasted_iota(jnp.int32, (seg_w, seg_w), 1)
    seg_ones = jnp.where(srow // n == scol // n, 1.0, 0.0).astype(BF16)

    def seg_sum(t, terms):
        pieces = _split(t, terms)
        return jnp.concatenate([sum(_dot(piece[:, j:j + seg_w], seg_ones) for piece in pieces)
                                for j in range(0, e, seg_w)], axis=-1)

    kmod = k * (1.0 + (a - 1.0) * ka_ref[...])
    e_neg = jnp.exp(-g)
    e_end = jnp.exp(g_end - g.reshape(nb, tc, e)).reshape(rows, e)
    kkr = k * kk_ref[...]
    kk = kkr / jnp.maximum(jnp.sqrt(seg_sum(kkr * kkr, 2)), 1e-12)
    bvec = kk * a
    hb = lambda t: heads(t.astype(BF16))
    v_h = hb(v)
    at = hb(-kk * jnp.exp(g - lw))
    rt = hb(r * jnp.exp(g))
    btkt = jnp.concatenate([hb(bvec * e_neg), hb(kmod * e_neg)], axis=1)
    bhkh = jnp.concatenate([hb(bvec * e_end), hb(kmod * e_end)], axis=1)
    s0 = s_ref[...]
    s0b = s0.astype(BF16)
    bdot = lambda spec, x, y: jnp.einsum(spec, x, y, preferred_element_type=F32)

    both = bdot("htk,hsk->hts", jnp.concatenate([at, rt], axis=1), jnp.concatenate([btkt, s0b], axis=1))
    aa, from_state = both[:, :, :2 * tc], both[:, :, 2 * tc:]
    row2 = lax.broadcasted_iota(jnp.int32, (tc, 2 * tc), 0)
    lane2 = lax.broadcasted_iota(jnp.int32, (tc, 2 * tc), 1)
    right = (lane2 >= tc)[None]
    col2 = jnp.where(lane2 >= tc, lane2 - tc, lane2)
    m_a = jnp.where((row2 > col2)[None], aa[:, :tc, :], 0.0).astype(BF16)
    m_r = jnp.where((row2 >= col2)[None], aa[:, tc:, :], 0.0).astype(BF16)

    rhs = from_state[:, :tc] + bdot(
        "hts,hsv->htv", m_a, jnp.concatenate([jnp.zeros_like(v_h), v_h], axis=1))
    pair = jnp.where(right, jnp.where(row2 == col2, 1.0, 0.0)[None], m_a.astype(F32))
    span = 1
    while span < tc:
        pair_b = pair.astype(BF16)
        pair = bdot("hts,hsu->htu", pair_b[:, :, :tc], pair_b) + jnp.where(right, pair, 0.0)
        span *= 2
    rhs_b = rhs.astype(BF16)
    sa = bdot("hts,hsv->htv", pair.astype(BF16), jnp.concatenate([jnp.zeros_like(rhs_b), rhs_b], axis=1))

    sv = jnp.concatenate([sa.astype(BF16), v_h], axis=1)
    out = from_state[:, tc:] + bdot("hts,hsv->htv", m_r, sv)
    s_ref[...] = s0 * heads(jnp.exp(g_end).reshape(nb, e)) + bdot("htv,htk->hvk", sv, bhkh)

    out = jnp.concatenate(
        [jnp.concatenate([out[b * nh + h] for h in range(nh)], axis=-1) for b in range(nb)], axis=0)
    dev = out - seg_sum(out, 1) * (1.0 / n)
    var = seg_sum(dev * dev, 1) * (1.0 / n)
    bonus = seg_sum(r * kmod * rk_ref[...], 1) * v
    res = dev * lax.rsqrt(var + RWKV_GN_EPS) * lgx_ref[...] + lbx_ref[...] + bonus

    y = _dot((res * _silu(z)).astype(BF16), wout_ref[...])
    pe = _dot(p_ref[0].reshape(rows, p_ref.shape[-1]).astype(BF16), wpe_ref[...])
    o_ref[...] = _tail(x, y, pe, wpg_ref, lng_ref, lnb_ref, alpha).reshape(nb, tc, d)


def _rwkv_layer(x, p, layer, mu, w_rkvz, w0, w1, w2, a0, a1, a2, k_k, k_a, r_k, lnx_g, lnx_b, w_out,
                wpe, wpg, ln_g, ln_b, alpha):
    b, l, d = x.shape
    e = w_out.shape[0]
    nh = e // RWKV_HEAD
    tc = min(RWKV_CHUNK, l)
    nb = RWKV_BATCH if b % RWKV_BATCH == 0 else 1
    row1 = lambda t: t.reshape(1, -1)
    chunk_rows = lambda width: pl.BlockSpec((nb, tc, width), lambda i, t: (i, t, 0))
    return pl.pallas_call(
        functools.partial(_rwkv_layer_kernel, alpha=alpha),
        grid=(b // nb, l // tc),
        in_specs=[chunk_rows(d), pl.BlockSpec((1, nb, tc, p.shape[-1]), lambda i, t: (layer, i, t, 0)),
                  _full(mu.shape), _full(w_rkvz.shape), _full((1, e)), _full(w1.shape), _full(w2.shape),
                  _full((1, e)), _full(a1.shape), _full(a2.shape)] + [_full((1, e))] * 5
                 + [_full(w_out.shape), _full(wpe.shape), _full(wpg.shape), _full((1, d)), _full((1, d))],
        out_specs=chunk_rows(d),
        out_shape=jax.ShapeDtypeStruct((b, l, d), F32),
        scratch_shapes=[pltpu.VMEM((nb * nh, RWKV_HEAD, RWKV_HEAD), F32),
                        pltpu.VMEM((nb, SUBLANES, d), F32)],
        compiler_params=_seq_params(),
        name="rwkv_layer",
    )(x, p, mu, w_rkvz.astype(BF16), row1(w0), w1.astype(BF16), w2.astype(BF16), row1(a0),
      a1.astype(BF16), a2.astype(BF16), row1(k_k), row1(k_a), row1(r_k), row1(lnx_g), row1(lnx_b),
      w_out.astype(BF16), wpe.astype(BF16), wpg.astype(BF16), row1(ln_g), row1(ln_b))


def kernel(x, p, conv_w_in, conv_k, conv_w_out, ssm_w_in, ssm_lam_re, ssm_lam_im, ssm_log_dt, ssm_b_re, ssm_b_im, ssm_c_re, ssm_c_im, ssm_d, ssm_w_glu, ssm_b_glu, ssm_w_out, rwkv_mu, rwkv_w_rkvz, rwkv_w0, rwkv_w1, rwkv_w2, rwkv_a0, rwkv_a1, rwkv_a2, rwkv_k_k, rwkv_k_a, rwkv_r_k, rwkv_lnx_g, rwkv_lnx_b, rwkv_w_out, ple_proj, ple_gate, ln_g, ln_b):
    depth = p.shape[0]
    alpha = (2 * depth) ** 0.25
    for i in range(depth):
        kind, j = i % 3, i // 3
        post = (ple_proj[i], ple_gate[i], ln_g[i], ln_b[i], alpha)
        if kind == 0:
            x = _conv_layer(x, p, i, conv_w_in[j], conv_k[j], conv_w_out[j], *post)
        elif kind == 1:
            x = _s5_layer(x, p, i, ssm_w_in[j], ssm_lam_re[j], ssm_lam_im[j], ssm_log_dt[j],
                          ssm_b_re[j], ssm_b_im[j], ssm_c_re[j], ssm_c_im[j], ssm_d[j],
                          ssm_w_glu[j], ssm_b_glu[j], ssm_w_out[j], *post)
        else:
            x = _rwkv_layer(x, p, i, rwkv_mu[j], rwkv_w_rkvz[j], rwkv_w0[j], rwkv_w1[j], rwkv_w2[j],
                            rwkv_a0[j], rwkv_a1[j], rwkv_a2[j], rwkv_k_k[j], rwkv_k_a[j],
                            rwkv_r_k[j].reshape(-1), rwkv_lnx_g[j], rwkv_lnx_b[j], rwkv_w_out[j], *post)
    return x
```

```python
import functools

import jax
import jax.numpy as jnp
from jax import lax
from jax.experimental import pallas as pl
from jax.experimental.pallas import tpu as pltpu

F32 = jnp.float32
BF16 = jnp.bfloat16

LN_EPS = 1e-5
RWKV_GN_EPS = 64e-5
RWKV_HEAD = 64
SSM_GROUP = 16
SUBLANES = 8
LANES = 128
SLAB_GROUPS = LANES // SSM_GROUP
VMEM_LIMIT = 56 * 1024 * 1024

ROW_TILE = 256
CONV_ROW_TILE = 512
CONV_SUBTILES = 4
RWKV_CHUNK = 64
RWKV_BATCH = 4
RWKV_GROUPS = 2


def _dot(a, b):
    return jnp.dot(a, b, preferred_element_type=F32)


def _split(t, terms):
    pieces = []
    for i in range(terms):
        pieces.append(t.astype(BF16))
        if i + 1 < terms:
            t = t - pieces[-1].astype(F32)
    return pieces


def _silu(z):
    return z * jax.nn.sigmoid(z)


def _shift_rows(cur, prev, shift):
    rolled = pltpu.roll(cur, shift, 0)
    rows = lax.broadcasted_iota(jnp.int32, (SUBLANES, 1), 0)
    top = jnp.where(rows >= shift, rolled[:SUBLANES], pltpu.roll(prev, shift, 0))
    return jnp.concatenate([top, rolled[SUBLANES:]], axis=0)


def _embed(p_ref, wpe_ref):
    return _dot(p_ref[0, 0].astype(BF16), wpe_ref[...])


def _tail(x, y, pe, wpg_ref, lng_ref, lnb_ref, alpha):
    r = alpha * x + y
    r = r + pe * jax.nn.sigmoid(_dot(r.astype(BF16), wpg_ref[...]))
    d = r - jnp.mean(r, axis=-1, keepdims=True)
    var = jnp.mean(d * d, axis=-1, keepdims=True)
    return d * lax.rsqrt(var + LN_EPS) * lng_ref[...] + lnb_ref[...]


def _conv_layer_kernel(x_ref, p_ref, win_ref, ck_ref, wout_ref, wpe_ref, wpg_ref, lng_ref, lnb_ref,
                       o_ref, carry_ref, *, alpha):
    e = wout_ref.shape[0]
    sub = x_ref.shape[1] // CONV_SUBTILES

    @pl.when(pl.program_id(1) == 0)
    def _():
        carry_ref[...] = jnp.zeros_like(carry_ref)

    def mix(i, prev):
        x = x_ref[0, i * sub:(i + 1) * sub, :]
        xb = x.astype(BF16)
        proj = lambda c: _dot(xb, win_ref[:, c * e:(c + 1) * e])
        u = proj(1) * proj(2)
        conv = (ck_ref[0:1, :] * _shift_rows(u, prev, 2) + ck_ref[1:2, :] * _shift_rows(u, prev, 1)
                + ck_ref[2:3, :] * u)
        return x, proj(0) * conv * _silu(proj(3)), u[sub - SUBLANES:, :]

    def finish(i, x, g):
        y = _dot(g.astype(BF16), wout_ref[...])
        pe = _dot(p_ref[0, 0, i * sub:(i + 1) * sub, :].astype(BF16), wpe_ref[...])
        o_ref[0, i * sub:(i + 1) * sub, :] = _tail(x, y, pe, wpg_ref, lng_ref, lnb_ref, alpha)

    prev, pending = carry_ref[...], None
    for i in range(CONV_SUBTILES):
        x, g, prev = mix(i, prev)
        if pending is not None:
            finish(*pending)
        pending = (i, x, g)
    finish(*pending)
    carry_ref[...] = prev


def _full(shape):
    return pl.BlockSpec(shape, lambda b, t: (0,) * len(shape))


def _rows(tm, width):
    return pl.BlockSpec((1, tm, width), lambda b, t: (b, t, 0))


def _layer_rows(layer, tm, width):
    return pl.BlockSpec((1, 1, tm, width), lambda b, t: (layer, b, t, 0))


def _seq_params():
    return pltpu.CompilerParams(dimension_semantics=("arbitrary", "arbitrary"),
                                vmem_limit_bytes=VMEM_LIMIT)


def _conv_layer(x, p, layer, w_in, conv_k, w_out, wpe, wpg, ln_g, ln_b, alpha):
    b, l, d = x.shape
    e = w_out.shape[0]
    tm = min(CONV_ROW_TILE, l)
    return pl.pallas_call(
        functools.partial(_conv_layer_kernel, alpha=alpha),
        grid=(b, l // tm),
        in_specs=[_rows(tm, d), _layer_rows(layer, tm, p.shape[-1]), _full(w_in.shape), _full(conv_k.shape),
                  _full(w_out.shape), _full(wpe.shape), _full(wpg.shape), _full((1, d)), _full((1, d))],
        out_specs=_rows(tm, d),
        out_shape=jax.ShapeDtypeStruct((b, l, d), F32),
        scratch_shapes=[pltpu.VMEM((SUBLANES, e), F32)],
        compiler_params=_seq_params(),
        name="conv_layer",
    )(x, p, w_in.astype(BF16), conv_k, w_out.astype(BF16), wpe.astype(BF16), wpg.astype(BF16),
      ln_g.reshape(1, d), ln_b.reshape(1, d))


def _s5_discretise_kernel(lre_ref, lim_ref, ldt_ref, bre_ref, bim_ref,
                          are_ref, aim_ref, pre_ref, pim_ref, bbre_ref, bbim_ref, *, seg_len):
    lre, lim = lre_ref[...], lim_ref[...]
    dt = jnp.exp(ldt_ref[...])
    mag = jnp.exp(lre * dt)
    are, aim = mag * jnp.cos(lim * dt), mag * jnp.sin(lim * dt)
    den = lre * lre + lim * lim
    cre = ((are - 1.0) * lre + aim * lim) / den
    cim = (aim * lre - (are - 1.0) * lim) / den
    bre, bim = bre_ref[...], bim_ref[...]
    bbre_ref[...] = cre[:, None, :] * bre - cim[:, None, :] * bim
    bbim_ref[...] = cre[:, None, :] * bim + cim[:, None, :] * bre
    are_ref[...] = are
    aim_ref[...] = aim
    sre, sim = are, aim
    for _ in range(seg_len.bit_length() - 1):
        sre, sim = sre * sre - sim * sim, 2.0 * sre * sim
    pr, pi = sre, sim
    for n in range(SUBLANES):
        pre_ref[n] = pr
        pim_ref[n] = pi
        pr, pi = pr * sre - pi * sim, pr * sim + pi * sre


def _s5_layer_kernel(x_ref, p_ref, win_ref, wbre_ref, wbim_ref, wcre_ref, wcim_ref, tab_ref, dsk_ref,
                     wglu_ref, bglu_ref, wout_ref, wpe_ref, wpg_ref, lng_ref, lnb_ref,
                     o_ref, state_ref, sre_ref, sim_ref, y_ref, *, alpha):
    e = wout_ref.shape[0]
    tm = x_ref.shape[1]
    nslab = wbre_ref.shape[0]
    w = wbre_ref.shape[2]
    seg_len = tm // SUBLANES

    @pl.when(pl.program_id(1) == 0)
    def _():
        state_ref[...] = jnp.zeros_like(state_ref)

    x = x_ref[0]
    xb = x.astype(BF16)
    u = _dot(xb, win_ref[:, :e])

    new_r = lax.broadcasted_iota(jnp.int32, (tm, tm), 0)
    old_r = lax.broadcasted_iota(jnp.int32, (tm, tm), 1)
    regroup = jnp.where(old_r == (new_r % SUBLANES) * seg_len + new_r // SUBLANES, 1.0, 0.0).astype(BF16)
    ungroup = jnp.where(new_r == (old_r % SUBLANES) * seg_len + old_r // SUBLANES, 1.0, 0.0).astype(BF16)
    ub = _dot(regroup, u.astype(BF16)).astype(BF16)
    rows = lax.broadcasted_iota(jnp.int32, (SUBLANES, 1), 0)

    def project_in(s):
        us = ub[:, s * LANES:(s + 1) * LANES]
        sre_ref[s] = _dot(us, wbre_ref[s])
        sim_ref[s] = _dot(us, wbim_ref[s])

    z_parts = []
    z_w = 2 * e // nslab
    project_in(0)
    for s in range(nslab):
        if s + 1 < nslab:
            project_in(s + 1)
        if s % 2 == 0:
            z_parts.append(_dot(xb, win_ref[:, e + (s // 2) * z_w:e + (s // 2 + 1) * z_w]))
        if s == 1:
            pe = _embed(p_ref, wpe_ref)
        sre, sim = sre_ref.at[s], sim_ref.at[s]
        ar, ai = tab_ref[s, 0], tab_ref[s, 1]

        def local_final(i, c):
            cr, ci = c
            r0 = pl.multiple_of(i * SUBLANES, SUBLANES)
            return (cr * ar - ci * ai + sre[pl.ds(r0, SUBLANES), :],
                    cr * ai + ci * ar + sim[pl.ds(r0, SUBLANES), :])

        zero = jnp.zeros((SUBLANES, w), F32)
        fr, fi = lax.fori_loop(0, seg_len, local_final, (zero, zero), unroll=True)
        for lvl, dist in enumerate((1, 2, 4)):
            pr, pi = tab_ref[s, 2 + 2 * lvl], tab_ref[s, 3 + 2 * lvl]
            gr, gi = pltpu.roll(fr, dist, 0), pltpu.roll(fi, dist, 0)
            fr, fi = fr + pr * gr - pi * gi, fi + pr * gi + pi * gr
        cr0, ci0 = state_ref[2 * s:2 * s + 1, :], state_ref[2 * s + 1:2 * s + 2, :]
        qr, qi = tab_ref[s, 8], tab_ref[s, 9]
        init_r = qr * cr0 - qi * ci0 + jnp.where(rows >= 1, pltpu.roll(fr, 1, 0), 0.0)
        init_i = qr * ci0 + qi * cr0 + jnp.where(rows >= 1, pltpu.roll(fi, 1, 0), 0.0)

        def all_states(i, c):
            cr, ci = c
            r0 = pl.multiple_of(i * SUBLANES, SUBLANES)
            nr = cr * ar - ci * ai + sre[pl.ds(r0, SUBLANES), :]
            ni = cr * ai + ci * ar + sim[pl.ds(r0, SUBLANES), :]
            sre[pl.ds(r0, SUBLANES), :] = nr
            sim[pl.ds(r0, SUBLANES), :] = ni
            return nr, ni

        lr, li = lax.fori_loop(0, seg_len, all_states, (init_r, init_i), unroll=True)
        state_ref[2 * s:2 * s + 1, :] = lr[SUBLANES - 1:, :]
        state_ref[2 * s + 1:2 * s + 2, :] = li[SUBLANES - 1:, :]
        y_ref[:, s * LANES:(s + 1) * LANES] = (_dot(sre[...].astype(BF16), wcre_ref[s])
                                               + _dot(sim[...].astype(BF16), wcim_ref[s]))

    y = sum(_dot(ungroup, piece) for piece in _split(y_ref[...], 2))

    y = jax.nn.gelu(y + dsk_ref[...] * u)
    y = y * jax.nn.sigmoid(_dot(y.astype(BF16), wglu_ref[...]) + bglu_ref[...])
    z = jnp.concatenate(z_parts, axis=-1)
    out = _dot((y * _silu(z)).astype(BF16), wout_ref[...])
    o_ref[0] = _tail(x, out, pe, wpg_ref, lng_ref, lnb_ref, alpha)


def _s5_layer(x, p, layer, w_in, lam_re, lam_im, log_dt, b_re, b_im, c_re, c_im, d_skip, w_glu, b_glu, w_out,
              wpe, wpg, ln_g, ln_b, alpha):
    b, l, d = x.shape
    e = w_out.shape[0]
    g, pst = lam_re.shape
    hch = b_re.shape[-1]
    nslab = g // SLAB_GROUPS
    w = SLAB_GROUPS * pst
    tm = min(ROW_TILE, l)
    seg_len = tm // SUBLANES
    assert seg_len & (seg_len - 1) == 0, "segment length must be a power of two"

    gp1 = jax.ShapeDtypeStruct((g, pst), F32)
    gp = jax.ShapeDtypeStruct((SUBLANES, g, pst), F32)
    gb = jax.ShapeDtypeStruct((g, hch, pst), F32)
    a_re, a_im, pw_re, pw_im, bb_re, bb_im = pl.pallas_call(
        functools.partial(_s5_discretise_kernel, seg_len=seg_len),
        out_shape=(gp1, gp1, gp, gp, gb, gb), name="s5_discretise",
    )(lam_re, lam_im, log_dt.reshape(g, 1), jnp.swapaxes(b_re, 1, 2), jnp.swapaxes(b_im, 1, 2))

    eye = jnp.eye(SLAB_GROUPS, dtype=F32)
    blk_in = lambda m: jnp.einsum("sghp,gk->sghkp", m.reshape(nslab, SLAB_GROUPS, hch, pst),
                                  eye).reshape(nslab, LANES, w).astype(BF16)
    blk_out = lambda m: jnp.einsum("sghp,gk->skpgh", m.reshape(nslab, SLAB_GROUPS, hch, pst),
                                   eye).reshape(nslab, w, LANES).astype(BF16)
    rows = jnp.arange(SUBLANES)[:, None, None]
    flat = lambda m: m.reshape(SUBLANES, nslab, w)
    tabs = [jnp.broadcast_to(m.reshape(1, nslab, w), (SUBLANES, nslab, w)) for m in (a_re, a_im)]
    for dist in (1, 2, 4):
        tabs += [jnp.where(rows >= dist, flat(m)[dist - 1][None], 0.0) for m in (pw_re, pw_im)]
    tabs.append(jnp.concatenate([jnp.ones((1, nslab, w), F32), flat(pw_re)[:SUBLANES - 1]]))
    tabs.append(jnp.concatenate([jnp.zeros((1, nslab, w), F32), flat(pw_im)[:SUBLANES - 1]]))
    tab = jnp.transpose(jnp.stack(tabs), (2, 0, 1, 3))

    return pl.pallas_call(
        functools.partial(_s5_layer_kernel, alpha=alpha),
        grid=(b, l // tm),
        in_specs=[_rows(tm, d), _layer_rows(layer, tm, p.shape[-1]), _full(w_in.shape),
                  _full((nslab, LANES, w)), _full((nslab, LANES, w)),
                  _full((nslab, w, LANES)), _full((nslab, w, LANES)),
                  _full(tab.shape), _full((1, e)),
                  _full(w_glu.shape), _full((1, e)), _full(w_out.shape), _full(wpe.shape),
                  _full(wpg.shape), _full((1, d)), _full((1, d))],
        out_specs=_rows(tm, d),
        out_shape=jax.ShapeDtypeStruct((b, l, d), F32),
        scratch_shapes=[pltpu.VMEM((2 * nslab, w), F32), pltpu.VMEM((nslab, tm, w), F32),
                        pltpu.VMEM((nslab, tm, w), F32), pltpu.VMEM((tm, e), F32)],
        compiler_params=_seq_params(),
        name="s5_layer",
    )(x, p, w_in.astype(BF16), blk_in(bb_re), blk_in(bb_im),
      blk_out(c_re), blk_out(-c_im),
      tab, d_skip.reshape(1, e), w_glu.astype(BF16), b_glu.reshape(1, e), w_out.astype(BF16),
      wpe.astype(BF16), wpg.astype(BF16), ln_g.reshape(1, d), ln_b.reshape(1, d))


def _rwkv_layer_kernel(x_ref, p_ref, mu_ref, wrkvz_ref, w0_ref, w1_ref, w2_ref, a0_ref, a1_ref, a2_ref,
                       kk_ref, ka_ref, rk_ref, lgx_ref, lbx_ref, wout_ref, wpe_ref, wpg_ref, lng_ref, lnb_ref,
                       o_ref, s_ref, carry_ref, *, alpha):
    nb, tc, d = x_ref.shape
    e = wout_ref.shape[0]
    n = s_ref.shape[1]
    nh = s_ref.shape[0] // nb
    rows = nb * tc

    @pl.when(pl.program_id(1) == 0)
    def _():
        s_ref[...] = jnp.zeros_like(s_ref)
        carry_ref[...] = jnp.zeros_like(carry_ref)

    x3 = x_ref[...]
    x = x3.reshape(rows, d)
    shifted = jnp.concatenate([_shift_rows(x3[b], carry_ref[b], 1) for b in range(nb)], axis=0)
    carry_ref[...] = x3[:, tc - SUBLANES:, :]
    dx = shifted - x
    mix = lambda i: (x + dx * mu_ref[i:i + 1, :]).astype(BF16)
    r, k, v, z = (_dot(mix(i), wrkvz_ref[i]) for i in range(4))
    lora_w = _dot(jnp.tanh(_dot(mix(4), w1_ref[...])).astype(BF16), w2_ref[...])
    w_log = -jax.nn.softplus(-(w0_ref[...] + lora_w)) - 0.5
    lw = -jnp.exp(w_log)
    lora_a = _dot(_dot(mix(5), a1_ref[...]).astype(BF16), a2_ref[...])
    a = jax.nn.sigmoid(a0_ref[...] + lora_a)

    seg_w = 2 * LANES
    srow = lax.broadcasted_iota(jnp.int32, (seg_w, seg_w), 0)
    scol = lax.broadcasted_iota(jnp.int32, (seg_w, seg_w), 1)
    seg_ones = jnp.where(srow // n == scol // n, 1.0, 0.0).astype(BF16)

    def seg_sum(t, terms):
        pieces = _split(t, terms)
        return jnp.concatenate([sum(_dot(piece[:, j:j + seg_w], seg_ones) for piece in pieces)
                                for j in range(0, e, seg_w)], axis=-1)

    ngroups = RWKV_GROUPS if nb % RWKV_GROUPS == 0 else 1
    gb = nb // ngroups
    grows = gb * tc
    heads = lambda t: jnp.stack([t[(t.shape[0] // gb) * b:(t.shape[0] // gb) * (b + 1), n * h:n * (h + 1)]
                                 for b in range(gb) for h in range(nh)])
    hb = lambda t: heads(t.astype(BF16))
    bdot = lambda spec, lhs, rhs: jnp.einsum(spec, lhs, rhs, preferred_element_type=F32)
    brow = lax.broadcasted_iota(jnp.int32, (grows, grows), 0)
    bcol = lax.broadcasted_iota(jnp.int32, (grows, grows), 1)
    tri = jnp.where((brow >= bcol) & (brow // tc == bcol // tc), 1.0, 0.0).astype(BF16)
    row2 = lax.broadcasted_iota(jnp.int32, (tc, 2 * tc), 0)
    lane2 = lax.broadcasted_iota(jnp.int32, (tc, 2 * tc), 1)
    right = (lane2 >= tc)[None]
    col2 = jnp.where(lane2 >= tc, lane2 - tc, lane2)

    def prepare(gi, ops):
        sl = slice(gi * grows, (gi + 1) * grows)
        rg, kg, vg, ag, lwg = r[sl], k[sl], v[sl], a[sl], lw[sl]
        g = sum(_dot(tri, piece) for piece in _split(lwg, 2))
        g_end = g.reshape(gb, tc, e)[:, tc - 1:, :]
        yield
        kmod = kg * (1.0 + (ag - 1.0) * ka_ref[...])
        e_neg = jnp.exp(-g)
        yield
        e_end = jnp.exp(g_end - g.reshape(gb, tc, e)).reshape(grows, e)
        kkr = kg * kk_ref[...]
        kk = kkr / jnp.maximum(jnp.sqrt(seg_sum(kkr * kkr, 2)), 1e-12)
        bvec = kk * ag
        yield
        ops["at"] = hb(-kk * jnp.exp(g - lwg))
        yield
        ops["rt"] = hb(rg * jnp.exp(g))
        yield
        ops["btkt"] = jnp.concatenate([hb(bvec * e_neg), hb(kmod * e_neg)], axis=1)
        yield
        ops["bhkh"] = jnp.concatenate([hb(bvec * e_end), hb(kmod * e_end)], axis=1)
        yield
        ops.update(v=vg, v_h=hb(vg), decay=heads(jnp.exp(g_end).reshape(gb, e)),
                   bonus=rg * kmod * rk_ref[...])

    def solve(gi, ops, res):
        v_h = ops["v_h"]
        hs = slice(gi * gb * nh, (gi + 1) * gb * nh)
        s0 = s_ref[hs]
        both = bdot("htk,hsk->hts", jnp.concatenate([ops["at"], ops["rt"]], axis=1),
                    jnp.concatenate([ops["btkt"], s0.astype(BF16)], axis=1))
        aa, from_state = both[:, :, :2 * tc], both[:, :, 2 * tc:]
        yield
        m_a = jnp.where((row2 > col2)[None], aa[:, :tc, :], 0.0).astype(BF16)
        m_r = jnp.where((row2 >= col2)[None], aa[:, tc:, :], 0.0).astype(BF16)
        rhs = from_state[:, :tc] + bdot(
            "hts,hsv->htv", m_a, jnp.concatenate([jnp.zeros_like(v_h), v_h], axis=1))
        yield
        pair = jnp.where(right, jnp.where(row2 == col2, 1.0, 0.0)[None], m_a.astype(F32))
        span = 1
        while span < tc:
            pair_b = pair.astype(BF16)
            pair = bdot("hts,hsu->htu", pair_b[:, :, :tc], pair_b) + jnp.where(right, pair, 0.0)
            span *= 2
            yield
        rhs_b = rhs.astype(BF16)
        sa = bdot("hts,hsv->htv", pair.astype(BF16), jnp.concatenate([jnp.zeros_like(rhs_b), rhs_b], axis=1))
        yield
        sv = jnp.concatenate([sa.astype(BF16), v_h], axis=1)
        out = from_state[:, tc:] + bdot("hts,hsv->htv", m_r, sv)
        s_ref[hs] = s0 * ops["decay"] + bdot("htv,htk->hvk", sv, ops["bhkh"])
        yield
        out = jnp.concatenate(
            [jnp.concatenate([out[b * nh + h] for h in range(nh)], axis=-1) for b in range(gb)], axis=0)
        dev = out - seg_sum(out, 1) * (1.0 / n)
        var = seg_sum(dev * dev, 1) * (1.0 / n)
        res.append(dev * lax.rsqrt(var + RWKV_GN_EPS) * lgx_ref[...] + lbx_ref[...]
                   + seg_sum(ops["bonus"], 1) * ops["v"])

    def alternate(*stages):
        stages = list(stages)
        while stages:
            for gen in list(stages):
                if next(gen, stages) is stages:
                    stages.remove(gen)

    ops = [dict() for _ in range(ngroups)]
    res = []
    alternate(prepare(0, ops[0]))
    for gi in range(ngroups):
        alternate(solve(gi, ops[gi], res), *([prepare(gi + 1, ops[gi + 1])] if gi + 1 < ngroups else []))
    res = jnp.concatenate(res, axis=0)

    y = _dot((res * _silu(z)).astype(BF16), wout_ref[...])
    pe = _dot(p_ref[0].reshape(rows, p_ref.shape[-1]).astype(BF16), wpe_ref[...])
    o_ref[...] = _tail(x, y, pe, wpg_ref, lng_ref, lnb_ref, alpha).reshape(nb, tc, d)


def _rwkv_layer(x, p, layer, mu, w_rkvz, w0, w1, w2, a0, a1, a2, k_k, k_a, r_k, lnx_g, lnx_b, w_out,
                wpe, wpg, ln_g, ln_b, alpha):
    b, l, d = x.shape
    e = w_out.shape[0]
    nh = e // RWKV_HEAD
    tc = min(RWKV_CHUNK, l)
    nb = RWKV_BATCH if b % RWKV_BATCH == 0 else 1
    row1 = lambda t: t.reshape(1, -1)
    chunk_rows = lambda width: pl.BlockSpec((nb, tc, width), lambda i, t: (i, t, 0))
    return pl.pallas_call(
        functools.partial(_rwkv_layer_kernel, alpha=alpha),
        grid=(b // nb, l // tc),
        in_specs=[chunk_rows(d), pl.BlockSpec((1, nb, tc, p.shape[-1]), lambda i, t: (layer, i, t, 0)),
                  _full(mu.shape), _full(w_rkvz.shape), _full((1, e)), _full(w1.shape), _full(w2.shape),
                  _full((1, e)), _full(a1.shape), _full(a2.shape)] + [_full((1, e))] * 5
                 + [_full(w_out.shape), _full(wpe.shape), _full(wpg.shape), _full((1, d)), _full((1, d))],
        out_specs=chunk_rows(d),
        out_shape=jax.ShapeDtypeStruct((b, l, d), F32),
        scratch_shapes=[pltpu.VMEM((nb * nh, RWKV_HEAD, RWKV_HEAD), F32),
                        pltpu.VMEM((nb, SUBLANES, d), F32)],
        compiler_params=_seq_params(),
        name="rwkv_layer",
    )(x, p, mu, w_rkvz.astype(BF16), row1(w0), w1.astype(BF16), w2.astype(BF16), row1(a0),
      a1.astype(BF16), a2.astype(BF16), row1(k_k), row1(k_a), row1(r_k), row1(lnx_g), row1(lnx_b),
      w_out.astype(BF16), wpe.astype(BF16), wpg.astype(BF16), row1(ln_g), row1(ln_b))


def kernel(x, p, conv_w_in, conv_k, conv_w_out, ssm_w_in, ssm_lam_re, ssm_lam_im, ssm_log_dt, ssm_b_re, ssm_b_im, ssm_c_re, ssm_c_im, ssm_d, ssm_w_glu, ssm_b_glu, ssm_w_out, rwkv_mu, rwkv_w_rkvz, rwkv_w0, rwkv_w1, rwkv_w2, rwkv_a0, rwkv_a1, rwkv_a2, rwkv_k_k, rwkv_k_a, rwkv_r_k, rwkv_lnx_g, rwkv_lnx_b, rwkv_w_out, ple_proj, ple_gate, ln_g, ln_b):
    depth = p.shape[0]
    alpha = (2 * depth) ** 0.25
    for i in range(depth):
        kind, j = i % 3, i // 3
        post = (ple_proj[i], ple_gate[i], ln_g[i], ln_b[i], alpha)
        if kind == 0:
            x = _conv_layer(x, p, i, conv_w_in[j], conv_k[j], conv_w_out[j], *post)
        elif kind == 1:
            x = _s5_layer(x, p, i, ssm_w_in[j], ssm_lam_re[j], ssm_lam_im[j], ssm_log_dt[j],
                          ssm_b_re[j], ssm_b_im[j], ssm_c_re[j], ssm_c_im[j], ssm_d[j],
                          ssm_w_glu[j], ssm_b_glu[j], ssm_w_out[j], *post)
        else:
            x = _rwkv_layer(x, p, i, rwkv_mu[j], rwkv_w_rkvz[j], rwkv_w0[j], rwkv_w1[j], rwkv_w2[j],
                            rwkv_a0[j], rwkv_a1[j], rwkv_a2[j], rwkv_k_k[j], rwkv_k_a[j],
                            rwkv_r_k[j].reshape(-1), rwkv_lnx_g[j], rwkv_lnx_b[j], rwkv_w_out[j], *post)
    return x
```

```python
import functools

import jax
import jax.numpy as jnp
from jax import lax
from jax.experimental import pallas as pl
from jax.experimental.pallas import tpu as pltpu

F32 = jnp.float32
BF16 = jnp.bfloat16

LN_EPS = 1e-5
RWKV_GN_EPS = 64e-5
RWKV_HEAD = 64
SSM_GROUP = 16
SUBLANES = 8
LANES = 128
SLAB_GROUPS = LANES // SSM_GROUP
VMEM_LIMIT = 56 * 1024 * 1024

ROW_TILE = 256
CONV_ROW_TILE = 512
CONV_SUBTILES = 4
RWKV_CHUNK = 64
RWKV_BATCH = 4
RWKV_GROUPS = 2


def _dot(a, b):
    return jnp.dot(a, b, preferred_element_type=F32)


def _split(t, terms):
    pieces = []
    for i in range(terms):
        pieces.append(t.astype(BF16))
        if i + 1 < terms:
            t = t - pieces[-1].astype(F32)
    return pieces


def _silu(z):
    return z * jax.nn.sigmoid(z)


def _shift_rows(cur, prev, shift):
    rolled = pltpu.roll(cur, shift, 0)
    rows = lax.broadcasted_iota(jnp.int32, (SUBLANES, 1), 0)
    top = jnp.where(rows >= shift, rolled[:SUBLANES], pltpu.roll(prev, shift, 0))
    return jnp.concatenate([top, rolled[SUBLANES:]], axis=0)


def _embed(p_ref, wpe_ref):
    return _dot(p_ref[0, 0].astype(BF16), wpe_ref[...])


def _tail(x, y, pe, wpg_ref, lng_ref, lnb_ref, alpha):
    r = alpha * x + y
    r = r + pe * jax.nn.sigmoid(_dot(r.astype(BF16), wpg_ref[...]))
    d = r - jnp.mean(r, axis=-1, keepdims=True)
    var = jnp.mean(d * d, axis=-1, keepdims=True)
    return d * lax.rsqrt(var + LN_EPS) * lng_ref[...] + lnb_ref[...]


def _conv_layer_kernel(x_ref, p_ref, win_ref, ck_ref, wout_ref, wpe_ref, wpg_ref, lng_ref, lnb_ref,
                       o_ref, carry_ref, *, alpha):
    e = wout_ref.shape[0]
    sub = x_ref.shape[1] // CONV_SUBTILES

    @pl.when(pl.program_id(1) == 0)
    def _():
        carry_ref[...] = jnp.zeros_like(carry_ref)

    def mix(i, prev):
        x = x_ref[0, i * sub:(i + 1) * sub, :]
        xb = x.astype(BF16)
        proj = lambda c: _dot(xb, win_ref[:, c * e:(c + 1) * e])
        u = proj(1) * proj(2)
        conv = (ck_ref[0:1, :] * _shift_rows(u, prev, 2) + ck_ref[1:2, :] * _shift_rows(u, prev, 1)
                + ck_ref[2:3, :] * u)
        return x, proj(0) * conv * _silu(proj(3)), u[sub - SUBLANES:, :]

    def finish(i, x, g):
        y = _dot(g.astype(BF16), wout_ref[...])
        pe = _dot(p_ref[0, 0, i * sub:(i + 1) * sub, :].astype(BF16), wpe_ref[...])
        o_ref[0, i * sub:(i + 1) * sub, :] = _tail(x, y, pe, wpg_ref, lng_ref, lnb_ref, alpha)

    prev, pending = carry_ref[...], None
    for i in range(CONV_SUBTILES):
        x, g, prev = mix(i, prev)
        if pending is not None:
            finish(*pending)
        pending = (i, x, g)
    finish(*pending)
    carry_ref[...] = prev


def _full(shape):
    return pl.BlockSpec(shape, lambda b, t: (0,) * len(shape))


def _rows(tm, width):
    return pl.BlockSpec((1, tm, width), lambda b, t: (b, t, 0))


def _layer_rows(layer, tm, width):
    return pl.BlockSpec((1, 1, tm, width), lambda b, t: (layer, b, t, 0))


def _seq_params():
    return pltpu.CompilerParams(dimension_semantics=("arbitrary", "arbitrary"),
                                vmem_limit_bytes=VMEM_LIMIT)


def _conv_layer(x, p, layer, w_in, conv_k, w_out, wpe, wpg, ln_g, ln_b, alpha):
    b, l, d = x.shape
    e = w_out.shape[0]
    tm = min(CONV_ROW_TILE, l)
    return pl.pallas_call(
        functools.partial(_conv_layer_kernel, alpha=alpha),
        grid=(b, l // tm),
        in_specs=[_rows(tm, d), _layer_rows(layer, tm, p.shape[-1]), _full(w_in.shape), _full(conv_k.shape),
                  _full(w_out.shape), _full(wpe.shape), _full(wpg.shape), _full((1, d)), _full((1, d))],
        out_specs=_rows(tm, d),
        out_shape=jax.ShapeDtypeStruct((b, l, d), F32),
        scratch_shapes=[pltpu.VMEM((SUBLANES, e), F32)],
        compiler_params=_seq_params(),
        name="conv_layer",
    )(x, p, w_in.astype(BF16), conv_k, w_out.astype(BF16), wpe.astype(BF16), wpg.astype(BF16),
      ln_g.reshape(1, d), ln_b.reshape(1, d))


def _s5_discretise_kernel(lre_ref, lim_ref, ldt_ref, bre_ref, bim_ref,
                          are_ref, aim_ref, pre_ref, pim_ref, bbre_ref, bbim_ref, *, seg_len):
    lre, lim = lre_ref[...], lim_ref[...]
    dt = jnp.exp(ldt_ref[...])
    mag = jnp.exp(lre * dt)
    are, aim = mag * jnp.cos(lim * dt), mag * jnp.sin(lim * dt)
    den = lre * lre + lim * lim
    cre = ((are - 1.0) * lre + aim * lim) / den
    cim = (aim * lre - (are - 1.0) * lim) / den
    bre, bim = bre_ref[...], bim_ref[...]
    bbre_ref[...] = cre[:, None, :] * bre - cim[:, None, :] * bim
    bbim_ref[...] = cre[:, None, :] * bim + cim[:, None, :] * bre
    are_ref[...] = are
    aim_ref[...] = aim
    sre, sim = are, aim
    for _ in range(seg_len.bit_length() - 1):
        sre, sim = sre * sre - sim * sim, 2.0 * sre * sim
    pr, pi = sre, sim
    for n in range(SUBLANES):
        pre_ref[n] = pr
        pim_ref[n] = pi
        pr, pi = pr * sre - pi * sim, pr * sim + pi * sre


def _s5_layer_kernel(x_ref, p_ref, win_ref, wbre_ref, wbim_ref, wcre_ref, wcim_ref, tab_ref, dsk_ref,
                     wglu_ref, bglu_ref, wout_ref, wpe_ref, wpg_ref, lng_ref, lnb_ref,
                     o_ref, state_ref, sre_ref, sim_ref, y_ref, *, alpha):
    e = wout_ref.shape[0]
    tm = x_ref.shape[1]
    nslab = wbre_ref.shape[0]
    w = wbre_ref.shape[2]
    seg_len = tm // SUBLANES

    @pl.when(pl.program_id(1) == 0)
    def _():
        state_ref[...] = jnp.zeros_like(state_ref)

    x = x_ref[0]
    xb = x.astype(BF16)
    u = _dot(xb, win_ref[:, :e])

    new_r = lax.broadcasted_iota(jnp.int32, (tm, tm), 0)
    old_r = lax.broadcasted_iota(jnp.int32, (tm, tm), 1)
    regroup = jnp.where(old_r == (new_r % SUBLANES) * seg_len + new_r // SUBLANES, 1.0, 0.0).astype(BF16)
    ungroup = jnp.where(new_r == (old_r % SUBLANES) * seg_len + old_r // SUBLANES, 1.0, 0.0).astype(BF16)
    ub = _dot(regroup, u.astype(BF16)).astype(BF16)
    rows = lax.broadcasted_iota(jnp.int32, (SUBLANES, 1), 0)

    def project_in(s):
        us = ub[:, s * LANES:(s + 1) * LANES]
        sre_ref[s] = _dot(us, wbre_ref[s])
        sim_ref[s] = _dot(us, wbim_ref[s])

    z_parts = []
    z_w = 2 * e // nslab
    project_in(0)
    for s in range(nslab):
        if s + 1 < nslab:
            project_in(s + 1)
        if s % 2 == 0:
            z_parts.append(_dot(xb, win_ref[:, e + (s // 2) * z_w:e + (s // 2 + 1) * z_w]))
        if s == 1:
            pe = _embed(p_ref, wpe_ref)
        sre, sim = sre_ref.at[s], sim_ref.at[s]
        ar, ai = tab_ref[s, 0], tab_ref[s, 1]

        def local_final(i, c):
            cr, ci = c
            r0 = pl.multiple_of(i * SUBLANES, SUBLANES)
            return (cr * ar - ci * ai + sre[pl.ds(r0, SUBLANES), :],
                    cr * ai + ci * ar + sim[pl.ds(r0, SUBLANES), :])

        zero = jnp.zeros((SUBLANES, w), F32)
        fr, fi = lax.fori_loop(0, seg_len, local_final, (zero, zero), unroll=True)
        for lvl, dist in enumerate((1, 2, 4)):
            pr, pi = tab_ref[s, 2 + 2 * lvl], tab_ref[s, 3 + 2 * lvl]
            gr, gi = pltpu.roll(fr, dist, 0), pltpu.roll(fi, dist, 0)
            fr, fi = fr + pr * gr - pi * gi, fi + pr * gi + pi * gr
        cr0, ci0 = state_ref[2 * s:2 * s + 1, :], state_ref[2 * s + 1:2 * s + 2, :]
        qr, qi = tab_ref[s, 8], tab_ref[s, 9]
        init_r = qr * cr0 - qi * ci0 + jnp.where(rows >= 1, pltpu.roll(fr, 1, 0), 0.0)
        init_i = qr * ci0 + qi * cr0 + jnp.where(rows >= 1, pltpu.roll(fi, 1, 0), 0.0)

        def all_states(i, c):
            cr, ci = c
            r0 = pl.multiple_of(i * SUBLANES, SUBLANES)
            nr = cr * ar - ci * ai + sre[pl.ds(r0, SUBLANES), :]
            ni = cr * ai + ci * ar + sim[pl.ds(r0, SUBLANES), :]
            sre[pl.ds(r0, SUBLANES), :] = nr
            sim[pl.ds(r0, SUBLANES), :] = ni
            return nr, ni

        lr, li = lax.fori_loop(0, seg_len, all_states, (init_r, init_i), unroll=True)
        state_ref[2 * s:2 * s + 1, :] = lr[SUBLANES - 1:, :]
        state_ref[2 * s + 1:2 * s + 2, :] = li[SUBLANES - 1:, :]
        y_ref[:, s * LANES:(s + 1) * LANES] = (_dot(sre[...].astype(BF16), wcre_ref[s])
                                               + _dot(sim[...].astype(BF16), wcim_ref[s]))

    y = sum(_dot(ungroup, piece) for piece in _split(y_ref[...], 2))

    y = jax.nn.gelu(y + dsk_ref[...] * u)
    y = y * jax.nn.sigmoid(_dot(y.astype(BF16), wglu_ref[...]) + bglu_ref[...])
    z = jnp.concatenate(z_parts, axis=-1)
    out = _dot((y * _silu(z)).astype(BF16), wout_ref[...])
    o_ref[0] = _tail(x, out, pe, wpg_ref, lng_ref, lnb_ref, alpha)


def _s5_layer(x, p, layer, w_in, lam_re, lam_im, log_dt, b_re, b_im, c_re, c_im, d_skip, w_glu, b_glu, w_out,
              wpe, wpg, ln_g, ln_b, alpha):
    b, l, d = x.shape
    e = w_out.shape[0]
    g, pst = lam_re.shape
    hch = b_re.shape[-1]
    nslab = g // SLAB_GROUPS
    w = SLAB_GROUPS * pst
    tm = min(ROW_TILE, l)
    seg_len = tm // SUBLANES
    assert seg_len & (seg_len - 1) == 0, "segment length must be a power of two"

    gp1 = jax.ShapeDtypeStruct((g, pst), F32)
    gp = jax.ShapeDtypeStruct((SUBLANES, g, pst), F32)
    gb = jax.ShapeDtypeStruct((g, hch, pst), F32)
    a_re, a_im, pw_re, pw_im, bb_re, bb_im = pl.pallas_call(
        functools.partial(_s5_discretise_kernel, seg_len=seg_len),
        out_shape=(gp1, gp1, gp, gp, gb, gb), name="s5_discretise",
    )(lam_re, lam_im, log_dt.reshape(g, 1), jnp.swapaxes(b_re, 1, 2), jnp.swapaxes(b_im, 1, 2))

    eye = jnp.eye(SLAB_GROUPS, dtype=F32)
    blk_in = lambda m: jnp.einsum("sghp,gk->sghkp", m.reshape(nslab, SLAB_GROUPS, hch, pst),
                                  eye).reshape(nslab, LANES, w).astype(BF16)
    blk_out = lambda m: jnp.einsum("sghp,gk->skpgh", m.reshape(nslab, SLAB_GROUPS, hch, pst),
                                   eye).reshape(nslab, w, LANES).astype(BF16)
    rows = jnp.arange(SUBLANES)[:, None, None]
    flat = lambda m: m.reshape(SUBLANES, nslab, w)
    tabs = [jnp.broadcast_to(m.reshape(1, nslab, w), (SUBLANES, nslab, w)) for m in (a_re, a_im)]
    for dist in (1, 2, 4):
        tabs += [jnp.where(rows >= dist, flat(m)[dist - 1][None], 0.0) for m in (pw_re, pw_im)]
    tabs.append(jnp.concatenate([jnp.ones((1, nslab, w), F32), flat(pw_re)[:SUBLANES - 1]]))
    tabs.append(jnp.concatenate([jnp.zeros((1, nslab, w), F32), flat(pw_im)[:SUBLANES - 1]]))
    tab = jnp.transpose(jnp.stack(tabs), (2, 0, 1, 3))

    return pl.pallas_call(
        functools.partial(_s5_layer_kernel, alpha=alpha),
        grid=(b, l // tm),
        in_specs=[_rows(tm, d), _layer_rows(layer, tm, p.shape[-1]), _full(w_in.shape),
                  _full((nslab, LANES, w)), _full((nslab, LANES, w)),
                  _full((nslab, w, LANES)), _full((nslab, w, LANES)),
                  _full(tab.shape), _full((1, e)),
                  _full(w_glu.shape), _full((1, e)), _full(w_out.shape), _full(wpe.shape),
                  _full(wpg.shape), _full((1, d)), _full((1, d))],
        out_specs=_rows(tm, d),
        out_shape=jax.ShapeDtypeStruct((b, l, d), F32),
        scratch_shapes=[pltpu.VMEM((2 * nslab, w), F32), pltpu.VMEM((nslab, tm, w), F32),
                        pltpu.VMEM((nslab, tm, w), F32), pltpu.VMEM((tm, e), F32)],
        compiler_params=_seq_params(),
        name="s5_layer",
    )(x, p, w_in.astype(BF16), blk_in(bb_re), blk_in(bb_im),
      blk_out(c_re), blk_out(-c_im),
      tab, d_skip.reshape(1, e), w_glu.astype(BF16), b_glu.reshape(1, e), w_out.astype(BF16),
      wpe.astype(BF16), wpg.astype(BF16), ln_g.reshape(1, d), ln_b.reshape(1, d))


def _rwkv_layer_kernel(x_ref, p_ref, mu_ref, wrkvz_ref, w0_ref, w1_ref, w2_ref, a0_ref, a1_ref, a2_ref,
                       kk_ref, ka_ref, rk_ref, lgx_ref, lbx_ref, wout_ref, wpe_ref, wpg_ref, lng_ref, lnb_ref,
                       o_ref, s_ref, carry_ref, *, alpha):
    nb, tc, d = x_ref.shape
    e = wout_ref.shape[0]
    n = s_ref.shape[1]
    nh = s_ref.shape[0] // nb
    rows = nb * tc

    @pl.when(pl.program_id(1) == 0)
    def _():
        s_ref[...] = jnp.zeros_like(s_ref)
        carry_ref[...] = jnp.zeros_like(carry_ref)

    proj = {}

    def project():
        x3 = x_ref[...]
        x = x3.reshape(rows, d)
        shifted = jnp.concatenate([_shift_rows(x3[b], carry_ref[b], 1) for b in range(nb)], axis=0)
        carry_ref[...] = x3[:, tc - SUBLANES:, :]
        dx = shifted - x
        mix = lambda i: (x + dx * mu_ref[i:i + 1, :]).astype(BF16)
        lora_w = _dot(jnp.tanh(_dot(mix(4), w1_ref[...])).astype(BF16), w2_ref[...])
        w_log = -jax.nn.softplus(-(w0_ref[...] + lora_w)) - 0.5
        proj.update(x=x, lw=-jnp.exp(w_log))
        yield
        proj["k"] = _dot(mix(1), wrkvz_ref[1])
        yield
        lora_a = _dot(_dot(mix(5), a1_ref[...]).astype(BF16), a2_ref[...])
        proj["a"] = jax.nn.sigmoid(a0_ref[...] + lora_a)
        yield
        proj["r"] = _dot(mix(0), wrkvz_ref[0])
        yield
        proj["v"] = _dot(mix(2), wrkvz_ref[2])
        yield
        proj["z"] = _dot(mix(3), wrkvz_ref[3])

    seg_w = 2 * LANES
    srow = lax.broadcasted_iota(jnp.int32, (seg_w, seg_w), 0)
    scol = lax.broadcasted_iota(jnp.int32, (seg_w, seg_w), 1)
    seg_ones = jnp.where(srow // n == scol // n, 1.0, 0.0).astype(BF16)

    def seg_sum(t, terms):
        pieces = _split(t, terms)
        return jnp.concatenate([sum(_dot(piece[:, j:j + seg_w], seg_ones) for piece in pieces)
                                for j in range(0, e, seg_w)], axis=-1)

    ngroups = RWKV_GROUPS if nb % RWKV_GROUPS == 0 else 1
    gb = nb // ngroups
    grows = gb * tc
    heads = lambda t: jnp.stack([t[(t.shape[0] // gb) * b:(t.shape[0] // gb) * (b + 1), n * h:n * (h + 1)]
                                 for b in range(gb) for h in range(nh)])
    hb = lambda t: heads(t.astype(BF16))
    bdot = lambda spec, lhs, rhs: jnp.einsum(spec, lhs, rhs, preferred_element_type=F32)
    brow = lax.broadcasted_iota(jnp.int32, (grows, grows), 0)
    bcol = lax.broadcasted_iota(jnp.int32, (grows, grows), 1)
    tri = jnp.where((brow >= bcol) & (brow // tc == bcol // tc), 1.0, 0.0).astype(BF16)
    row2 = lax.broadcasted_iota(jnp.int32, (tc, 2 * tc), 0)
    lane2 = lax.broadcasted_iota(jnp.int32, (tc, 2 * tc), 1)
    right = (lane2 >= tc)[None]
    col2 = jnp.where(lane2 >= tc, lane2 - tc, lane2)

    def prepare(gi, ops):
        sl = slice(gi * grows, (gi + 1) * grows)
        kg, ag, lwg = proj["k"][sl], proj["a"][sl], proj["lw"][sl]
        g = sum(_dot(tri, piece) for piece in _split(lwg, 2))
        g_end = g.reshape(gb, tc, e)[:, tc - 1:, :]
        yield
        kmod = kg * (1.0 + (ag - 1.0) * ka_ref[...])
        e_neg = jnp.exp(-g)
        yield
        e_end = jnp.exp(g_end - g.reshape(gb, tc, e)).reshape(grows, e)
        kkr = kg * kk_ref[...]
        kk = kkr / jnp.maximum(jnp.sqrt(seg_sum(kkr * kkr, 2)), 1e-12)
        bvec = kk * ag
        yield
        ops["at"] = hb(-kk * jnp.exp(g - lwg))
        yield
        rg = proj["r"][sl]
        ops["rt"] = hb(rg * jnp.exp(g))
        yield
        ops["btkt"] = jnp.concatenate([hb(bvec * e_neg), hb(kmod * e_neg)], axis=1)
        yield
        ops["bhkh"] = jnp.concatenate([hb(bvec * e_end), hb(kmod * e_end)], axis=1)
        yield
        vg = proj["v"][sl]
        ops.update(v=vg, v_h=hb(vg), decay=heads(jnp.exp(g_end).reshape(gb, e)),
                   bonus=rg * kmod * rk_ref[...])

    def solve(gi, ops):
        v_h = ops["v_h"]
        hs = slice(gi * gb * nh, (gi + 1) * gb * nh)
        s0 = s_ref[hs]
        both = bdot("htk,hsk->hts", jnp.concatenate([ops["at"], ops["rt"]], axis=1),
                    jnp.concatenate([ops["btkt"], s0.astype(BF16)], axis=1))
        aa, from_state = both[:, :, :2 * tc], both[:, :, 2 * tc:]
        yield
        m_a = jnp.where((row2 > col2)[None], aa[:, :tc, :], 0.0).astype(BF16)
        m_r = jnp.where((row2 >= col2)[None], aa[:, tc:, :], 0.0).astype(BF16)
        rhs = from_state[:, :tc] + bdot(
            "hts,hsv->htv", m_a, jnp.concatenate([jnp.zeros_like(v_h), v_h], axis=1))
        yield
        pair = jnp.where(right, jnp.where(row2 == col2, 1.0, 0.0)[None], m_a.astype(F32))
        span = 1
        while span < tc:
            pair_b = pair.astype(BF16)
            pair = bdot("hts,hsu->htu", pair_b[:, :, :tc], pair_b) + jnp.where(right, pair, 0.0)
            span *= 2
            yield
        rhs_b = rhs.astype(BF16)
        sa = bdot("hts,hsv->htv", pair.astype(BF16), jnp.concatenate([jnp.zeros_like(rhs_b), rhs_b], axis=1))
        yield
        sv = jnp.concatenate([sa.astype(BF16), v_h], axis=1)
        out = from_state[:, tc:] + bdot("hts,hsv->htv", m_r, sv)
        s_ref[hs] = s0 * ops["decay"] + bdot("htv,htk->hvk", sv, ops["bhkh"])
        yield
        out = jnp.concatenate(
            [jnp.concatenate([out[b * nh + h] for h in range(nh)], axis=-1) for b in range(gb)], axis=0)
        dev = out - seg_sum(out, 1) * (1.0 / n)
        var = seg_sum(dev * dev, 1) * (1.0 / n)
        ops["res"] = (dev * lax.rsqrt(var + RWKV_GN_EPS) * lgx_ref[...] + lbx_ref[...]
                      + seg_sum(ops["bonus"], 1) * ops["v"])

    def finish(gi, ops):
        sl, bs = slice(gi * grows, (gi + 1) * grows), slice(gi * gb, (gi + 1) * gb)
        y = _dot((ops["res"] * _silu(proj["z"][sl])).astype(BF16), wout_ref[...])
        yield
        pe = _dot(p_ref[0, bs].reshape(grows, p_ref.shape[-1]).astype(BF16), wpe_ref[...])
        o_ref[bs] = _tail(proj["x"][sl], y, pe, wpg_ref, lng_ref, lnb_ref, alpha).reshape(gb, tc, d)

    def alternate(*stages):
        stages = list(stages)
        while stages:
            for gen in list(stages):
                if next(gen, stages) is stages:
                    stages.remove(gen)

    ops = [dict() for _ in range(ngroups)]
    projecting = project()
    for _ in range(3):
        next(projecting)
    alternate(projecting, prepare(0, ops[0]))
    for gi in range(ngroups):
        alternate(solve(gi, ops[gi]),
                  *([prepare(gi + 1, ops[gi + 1])] if gi + 1 < ngroups else []),
                  *([finish(gi - 1, ops[gi - 1])] if gi >= 1 else []))
    alternate(finish(ngroups - 1, ops[ngroups - 1]))


def _rwkv_layer(x, p, layer, mu, w_rkvz, w0, w1, w2, a0, a1, a2, k_k, k_a, r_k, lnx_g, lnx_b, w_out,
                wpe, wpg, ln_g, ln_b, alpha):
    b, l, d = x.shape
    e = w_out.shape[0]
    nh = e // RWKV_HEAD
    tc = min(RWKV_CHUNK, l)
    nb = RWKV_BATCH if b % RWKV_BATCH == 0 else 1
    row1 = lambda t: t.reshape(1, -1)
    chunk_rows = lambda width: pl.BlockSpec((nb, tc, width), lambda i, t: (i, t, 0))
    return pl.pallas_call(
        functools.partial(_rwkv_layer_kernel, alpha=alpha),
        grid=(b // nb, l // tc),
        in_specs=[chunk_rows(d), pl.BlockSpec((1, nb, tc, p.shape[-1]), lambda i, t: (layer, i, t, 0)),
                  _full(mu.shape), _full(w_rkvz.shape), _full((1, e)), _full(w1.shape), _full(w2.shape),
                  _full((1, e)), _full(a1.shape), _full(a2.shape)] + [_full((1, e))] * 5
                 + [_full(w_out.shape), _full(wpe.shape), _full(wpg.shape), _full((1, d)), _full((1, d))],
        out_specs=chunk_rows(d),
        out_shape=jax.ShapeDtypeStruct((b, l, d), F32),
        scratch_shapes=[pltpu.VMEM((nb * nh, RWKV_HEAD, RWKV_HEAD), F32),
                        pltpu.VMEM((nb, SUBLANES, d), F32)],
        compiler_params=_seq_params(),
        name="rwkv_layer",
    )(x, p, mu, w_rkvz.astype(BF16), row1(w0), w1.astype(BF16), w2.astype(BF16), row1(a0),
      a1.astype(BF16), a2.astype(BF16), row1(k_k), row1(k_a), row1(r_k), row1(lnx_g), row1(lnx_b),
      w_out.astype(BF16), wpe.astype(BF16), wpg.astype(BF16), row1(ln_g), row1(ln_b))


def kernel(x, p, conv_w_in, conv_k, conv_w_out, ssm_w_in, ssm_lam_re, ssm_lam_im, ssm_log_dt, ssm_b_re, ssm_b_im, ssm_c_re, ssm_c_im, ssm_d, ssm_w_glu, ssm_b_glu, ssm_w_out, rwkv_mu, rwkv_w_rkvz, rwkv_w0, rwkv_w1, rwkv_w2, rwkv_a0, rwkv_a1, rwkv_a2, rwkv_k_k, rwkv_k_a, rwkv_r_k, rwkv_lnx_g, rwkv_lnx_b, rwkv_w_out, ple_proj, ple_gate, ln_g, ln_b):
    depth = p.shape[0]
    alpha = (2 * depth) ** 0.25
    for i in range(depth):
        kind, j = i % 3, i // 3
        post = (ple_proj[i], ple_gate[i], ln_g[i], ln_b[i], alpha)
        if kind == 0:
            x = _conv_layer(x, p, i, conv_w_in[j], conv_k[j], conv_w_out[j], *post)
        elif kind == 1:
            x = _s5_layer(x, p, i, ssm_w_in[j], ssm_lam_re[j], ssm_lam_im[j], ssm_log_dt[j],
                          ssm_b_re[j], ssm_b_im[j], ssm_c_re[j], ssm_c_im[j], ssm_d[j],
                          ssm_w_glu[j], ssm_b_glu[j], ssm_w_out[j], *post)
        else:
            x = _rwkv_layer(x, p, i, rwkv_mu[j], rwkv_w_rkvz[j], rwkv_w0[j], rwkv_w1[j], rwkv_w2[j],
                            rwkv_a0[j], rwkv_a1[j], rwkv_a2[j], rwkv_k_k[j], rwkv_k_a[j],
                            rwkv_r_k[j].reshape(-1), rwkv_lnx_g[j], rwkv_lnx_b[j], rwkv_w_out[j], *post)
    return x
```

```python
import functools

import jax
import jax.numpy as jnp
from jax import lax
from jax.experimental import pallas as pl
from jax.experimental.pallas import tpu as pltpu

F32 = jnp.float32
BF16 = jnp.bfloat16

LN_EPS = 1e-5
RWKV_GN_EPS = 64e-5
RWKV_HEAD = 64
SSM_GROUP = 16
SUBLANES = 8
LANES = 128
SLAB_GROUPS = LANES // SSM_GROUP
VMEM_LIMIT = 56 * 1024 * 1024

ROW_TILE = 256
CONV_ROW_TILE = 512
CONV_SUBTILES = 4
RWKV_CHUNK = 64
RWKV_BATCH = 4
RWKV_GROUPS = 2


def _dot(a, b):
    return jnp.dot(a, b, preferred_element_type=F32)


def _split(t, terms):
    pieces = []
    for i in range(terms):
        pieces.append(t.astype(BF16))
        if i + 1 < terms:
            t = t - pieces[-1].astype(F32)
    return pieces


def _alternate(*stages):
    stages = list(stages)
    while stages:
        for gen in list(stages):
            if next(gen, stages) is stages:
                stages.remove(gen)


def _silu(z):
    return z * jax.nn.sigmoid(z)


def _shift_rows(cur, prev, shift):
    rolled = pltpu.roll(cur, shift, 0)
    rows = lax.broadcasted_iota(jnp.int32, (SUBLANES, 1), 0)
    top = jnp.where(rows >= shift, rolled[:SUBLANES], pltpu.roll(prev, shift, 0))
    return jnp.concatenate([top, rolled[SUBLANES:]], axis=0)


def _embed(p_ref, wpe_ref):
    return _dot(p_ref[0, 0].astype(BF16), wpe_ref[...])


def _tail(x, y, pe, wpg_ref, lng_ref, lnb_ref, alpha):
    r = alpha * x + y
    r = r + pe * jax.nn.sigmoid(_dot(r.astype(BF16), wpg_ref[...]))
    d = r - jnp.mean(r, axis=-1, keepdims=True)
    var = jnp.mean(d * d, axis=-1, keepdims=True)
    return d * lax.rsqrt(var + LN_EPS) * lng_ref[...] + lnb_ref[...]


def _conv_layer_kernel(x_ref, p_ref, win_ref, ck_ref, wout_ref, wpe_ref, wpg_ref, lng_ref, lnb_ref,
                       o_ref, carry_ref, *, alpha):
    e = wout_ref.shape[0]
    sub = x_ref.shape[1] // CONV_SUBTILES

    @pl.when(pl.program_id(1) == 0)
    def _():
        carry_ref[...] = jnp.zeros_like(carry_ref)

    def mix(i, st):
        x = x_ref[0, i * sub:(i + 1) * sub, :]
        xb = x.astype(BF16)
        proj = lambda c: _dot(xb, win_ref[:, c * e:(c + 1) * e])
        u = proj(1) * proj(2)
        st["next"] = u[sub - SUBLANES:, :]
        yield
        prev = st["prev"]
        conv = (ck_ref[0:1, :] * _shift_rows(u, prev, 2) + ck_ref[1:2, :] * _shift_rows(u, prev, 1)
                + ck_ref[2:3, :] * u)
        gated = proj(0) * conv
        yield
        st.update(x=x, g=gated * _silu(proj(3)))

    def finish(i, st):
        y = _dot(st["g"].astype(BF16), wout_ref[...])
        yield
        pe = _dot(p_ref[0, 0, i * sub:(i + 1) * sub, :].astype(BF16), wpe_ref[...])
        o_ref[0, i * sub:(i + 1) * sub, :] = _tail(st["x"], y, pe, wpg_ref, lng_ref, lnb_ref, alpha)

    sts = [dict() for _ in range(CONV_SUBTILES)]
    sts[0]["prev"] = carry_ref[...]
    _alternate(mix(0, sts[0]))
    for i in range(1, CONV_SUBTILES):
        sts[i]["prev"] = sts[i - 1]["next"]
        _alternate(mix(i, sts[i]), finish(i - 1, sts[i - 1]))
    _alternate(finish(CONV_SUBTILES - 1, sts[-1]))
    carry_ref[...] = sts[-1]["next"]


def _full(shape):
    return pl.BlockSpec(shape, lambda b, t: (0,) * len(shape))


def _rows(tm, width):
    return pl.BlockSpec((1, tm, width), lambda b, t: (b, t, 0))


def _layer_rows(layer, tm, width):
    return pl.BlockSpec((1, 1, tm, width), lambda b, t: (layer, b, t, 0))


def _seq_params():
    return pltpu.CompilerParams(dimension_semantics=("arbitrary", "arbitrary"),
                                vmem_limit_bytes=VMEM_LIMIT)


def _conv_layer(x, p, layer, w_in, conv_k, w_out, wpe, wpg, ln_g, ln_b, alpha):
    b, l, d = x.shape
    e = w_out.shape[0]
    tm = min(CONV_ROW_TILE, l)
    return pl.pallas_call(
        functools.partial(_conv_layer_kernel, alpha=alpha),
        grid=(b, l // tm),
        in_specs=[_rows(tm, d), _layer_rows(layer, tm, p.shape[-1]), _full(w_in.shape), _full(conv_k.shape),
                  _full(w_out.shape), _full(wpe.shape), _full(wpg.shape), _full((1, d)), _full((1, d))],
        out_specs=_rows(tm, d),
        out_shape=jax.ShapeDtypeStruct((b, l, d), F32),
        scratch_shapes=[pltpu.VMEM((SUBLANES, e), F32)],
        compiler_params=_seq_params(),
        name="conv_layer",
    )(x, p, w_in.astype(BF16), conv_k, w_out.astype(BF16), wpe.astype(BF16), wpg.astype(BF16),
      ln_g.reshape(1, d), ln_b.reshape(1, d))


def _s5_discretise_kernel(lre_ref, lim_ref, ldt_ref, bre_ref, bim_ref, tab_ref, bbre_ref, bbim_ref,
                          *, seg_len):
    lre, lim = lre_ref[...], lim_ref[...]
    dt = jnp.exp(ldt_ref[...])
    mag = jnp.exp(lre * dt)
    are, aim = mag * jnp.cos(lim * dt), mag * jnp.sin(lim * dt)
    den = lre * lre + lim * lim
    cre = ((are - 1.0) * lre + aim * lim) / den
    cim = (aim * lre - (are - 1.0) * lim) / den
    bre, bim = bre_ref[...], bim_ref[...]
    bbre_ref[...] = cre[:, None, :] * bre - cim[:, None, :] * bim
    bbim_ref[...] = cre[:, None, :] * bim + cim[:, None, :] * bre
    sre, sim = are, aim
    for _ in range(seg_len.bit_length() - 1):
        sre, sim = sre * sre - sim * sim, 2.0 * sre * sim
    powers = [(jnp.ones_like(sre), jnp.zeros_like(sim))]
    for _ in range(SUBLANES - 1):
        pr, pi = powers[-1]
        powers.append((pr * sre - pi * sim, pr * sim + pi * sre))
    zero = jnp.zeros_like(are)
    for row in range(SUBLANES):
        tab_ref[0, row], tab_ref[1, row] = are, aim
        for lvl, dist in enumerate((1, 2, 4)):
            pr, pi = powers[dist] if row >= dist else (zero, zero)
            tab_ref[2 + 2 * lvl, row], tab_ref[3 + 2 * lvl, row] = pr, pi
        tab_ref[8, row], tab_ref[9, row] = powers[row]


def _s5_layer_kernel(x_ref, p_ref, win_ref, wbre_ref, wbim_ref, wcre_ref, wcim_ref, tab_ref, dsk_ref,
                     wglu_ref, bglu_ref, wout_ref, wpe_ref, wpg_ref, lng_ref, lnb_ref,
                     o_ref, state_ref, sre_ref, sim_ref, y_ref, *, alpha):
    e = wout_ref.shape[0]
    tm = x_ref.shape[1]
    nslab = wbre_ref.shape[0]
    w = wbre_ref.shape[2]
    seg_len = tm // SUBLANES

    @pl.when(pl.program_id(1) == 0)
    def _():
        state_ref[...] = jnp.zeros_like(state_ref)

    x = x_ref[0]
    xb = x.astype(BF16)
    u = _dot(xb, win_ref[:, :e])

    new_r = lax.broadcasted_iota(jnp.int32, (tm, tm), 0)
    old_r = lax.broadcasted_iota(jnp.int32, (tm, tm), 1)
    regroup = jnp.where(old_r == (new_r % SUBLANES) * seg_len + new_r // SUBLANES, 1.0, 0.0).astype(BF16)
    ungroup = jnp.where(new_r == (old_r % SUBLANES) * seg_len + old_r // SUBLANES, 1.0, 0.0).astype(BF16)
    ub = _dot(regroup, u.astype(BF16)).astype(BF16)
    rows = lax.broadcasted_iota(jnp.int32, (SUBLANES, 1), 0)

    def project_in(s):
        us = ub[:, s * LANES:(s + 1) * LANES]
        sre_ref[s] = _dot(us, wbre_ref[s])
        sim_ref[s] = _dot(us, wbim_ref[s])

    z_parts = []
    z_w = 2 * e // nslab
    project_in(0)
    for s in range(nslab):
        if s + 1 < nslab:
            project_in(s + 1)
        if s % 2 == 0:
            z_parts.append(_dot(xb, win_ref[:, e + (s // 2) * z_w:e + (s // 2 + 1) * z_w]))
        if s == 1:
            pe = _embed(p_ref, wpe_ref)
        sre, sim = sre_ref.at[s], sim_ref.at[s]
        ar, ai = tab_ref[s, 0], tab_ref[s, 1]

        def local_final(i, c):
            cr, ci = c
            r0 = pl.multiple_of(i * SUBLANES, SUBLANES)
            return (cr * ar - ci * ai + sre[pl.ds(r0, SUBLANES), :],
                    cr * ai + ci * ar + sim[pl.ds(r0, SUBLANES), :])

        zero = jnp.zeros((SUBLANES, w), F32)
        fr, fi = lax.fori_loop(0, seg_len, local_final, (zero, zero), unroll=True)
        for lvl, dist in enumerate((1, 2, 4)):
            pr, pi = tab_ref[s, 2 + 2 * lvl], tab_ref[s, 3 + 2 * lvl]
            gr, gi = pltpu.roll(fr, dist, 0), pltpu.roll(fi, dist, 0)
            fr, fi = fr + pr * gr - pi * gi, fi + pr * gi + pi * gr
        cr0, ci0 = state_ref[2 * s:2 * s + 1, :], state_ref[2 * s + 1:2 * s + 2, :]
        qr, qi = tab_ref[s, 8], tab_ref[s, 9]
        init_r = qr * cr0 - qi * ci0 + jnp.where(rows >= 1, pltpu.roll(fr, 1, 0), 0.0)
        init_i = qr * ci0 + qi * cr0 + jnp.where(rows >= 1, pltpu.roll(fi, 1, 0), 0.0)

        def all_states(i, c):
            cr, ci = c
            r0 = pl.multiple_of(i * SUBLANES, SUBLANES)
            nr = cr * ar - ci * ai + sre[pl.ds(r0, SUBLANES), :]
            ni = cr * ai + ci * ar + sim[pl.ds(r0, SUBLANES), :]
            sre[pl.ds(r0, SUBLANES), :] = nr
            sim[pl.ds(r0, SUBLANES), :] = ni
            return nr, ni

        lr, li = lax.fori_loop(0, seg_len, all_states, (init_r, init_i), unroll=True)
        state_ref[2 * s:2 * s + 1, :] = lr[SUBLANES - 1:, :]
        state_ref[2 * s + 1:2 * s + 2, :] = li[SUBLANES - 1:, :]
        y_ref[:, s * LANES:(s + 1) * LANES] = (_dot(sre[...].astype(BF16), wcre_ref[s])
                                               + _dot(sim[...].astype(BF16), wcim_ref[s]))

    y = sum(_dot(ungroup, piece) for piece in _split(y_ref[...], 2))

    y = jax.nn.gelu(y + dsk_ref[...] * u)
    y = y * jax.nn.sigmoid(_dot(y.astype(BF16), wglu_ref[...]) + bglu_ref[...])
    z = jnp.concatenate(z_parts, axis=-1)
    out = _dot((y * _silu(z)).astype(BF16), wout_ref[...])
    o_ref[0] = _tail(x, out, pe, wpg_ref, lng_ref, lnb_ref, alpha)


def _s5_layer(x, p, layer, w_in, lam_re, lam_im, log_dt, b_re, b_im, c_re, c_im, d_skip, w_glu, b_glu, w_out,
              wpe, wpg, ln_g, ln_b, alpha):
    b, l, d = x.shape
    e = w_out.shape[0]
    g, pst = lam_re.shape
    hch = b_re.shape[-1]
    nslab = g // SLAB_GROUPS
    w = SLAB_GROUPS * pst
    tm = min(ROW_TILE, l)
    seg_len = tm // SUBLANES
    assert seg_len & (seg_len - 1) == 0, "segment length must be a power of two"

    gb = jax.ShapeDtypeStruct((g, hch, pst), F32)
    tab, bb_re, bb_im = pl.pallas_call(
        functools.partial(_s5_discretise_kernel, seg_len=seg_len),
        out_shape=(jax.ShapeDtypeStruct((10, SUBLANES, g, pst), F32), gb, gb), name="s5_discretise",
    )(lam_re, lam_im, log_dt.reshape(g, 1), jnp.swapaxes(b_re, 1, 2), jnp.swapaxes(b_im, 1, 2))
    tab = jnp.transpose(tab.reshape(10, SUBLANES, nslab, w), (2, 0, 1, 3))

    eye = jnp.eye(SLAB_GROUPS, dtype=F32)
    blk_in = lambda m: jnp.einsum("sghp,gk->sghkp", m.reshape(nslab, SLAB_GROUPS, hch, pst),
                                  eye).reshape(nslab, LANES, w).astype(BF16)
    blk_out = lambda m: jnp.einsum("sghp,gk->skpgh", m.reshape(nslab, SLAB_GROUPS, hch, pst),
                                   eye).reshape(nslab, w, LANES).astype(BF16)
    return pl.pallas_call(
        functools.partial(_s5_layer_kernel, alpha=alpha),
        grid=(b, l // tm),
        in_specs=[_rows(tm, d), _layer_rows(layer, tm, p.shape[-1]), _full(w_in.shape),
                  _full((nslab, LANES, w)), _full((nslab, LANES, w)),
                  _full((nslab, w, LANES)), _full((nslab, w, LANES)),
                  _full(tab.shape), _full((1, e)),
                  _full(w_glu.shape), _full((1, e)), _full(w_out.shape), _full(wpe.shape),
                  _full(wpg.shape), _full((1, d)), _full((1, d))],
        out_specs=_rows(tm, d),
        out_shape=jax.ShapeDtypeStruct((b, l, d), F32),
        scratch_shapes=[pltpu.VMEM((2 * nslab, w), F32), pltpu.VMEM((nslab, tm, w), F32),
                        pltpu.VMEM((nslab, tm, w), F32), pltpu.VMEM((tm, e), F32)],
        compiler_params=_seq_params(),
        name="s5_layer",
    )(x, p, w_in.astype(BF16), blk_in(bb_re), blk_in(bb_im),
      blk_out(c_re), blk_out(-c_im),
      tab, d_skip.reshape(1, e), w_glu.astype(BF16), b_glu.reshape(1, e), w_out.astype(BF16),
      wpe.astype(BF16), wpg.astype(BF16), ln_g.reshape(1, d), ln_b.reshape(1, d))


def _rwkv_layer_kernel(x_ref, p_ref, mu_ref, wrkvz_ref, w0_ref, w1_ref, w2_ref, a0_ref, a1_ref, a2_ref,
                       kk_ref, ka_ref, rk_ref, lgx_ref, lbx_ref, wout_ref, wpe_ref, wpg_ref, lng_ref, lnb_ref,
                       o_ref, s_ref, carry_ref, *, alpha):
    nb, tc, d = x_ref.shape
    e = wout_ref.shape[0]
    n = s_ref.shape[1]
    nh = s_ref.shape[0] // nb
    rows = nb * tc

    @pl.when(pl.program_id(1) == 0)
    def _():
        s_ref[...] = jnp.zeros_like(s_ref)
        carry_ref[...] = jnp.zeros_like(carry_ref)

    proj = {}

    def project():
        x3 = x_ref[...]
        x = x3.reshape(rows, d)
        shifted = jnp.concatenate([_shift_rows(x3[b], carry_ref[b], 1) for b in range(nb)], axis=0)
        carry_ref[...] = x3[:, tc - SUBLANES:, :]
        dx = shifted - x
        mix = lambda i: (x + dx * mu_ref[i:i + 1, :]).astype(BF16)
        lora_w = _dot(jnp.tanh(_dot(mix(4), w1_ref[...])).astype(BF16), w2_ref[...])
        w_log = -jax.nn.softplus(-(w0_ref[...] + lora_w)) - 0.5
        proj.update(x=x, lw=-jnp.exp(w_log))
        yield
        proj["k"] = _dot(mix(1), wrkvz_ref[1])
        yield
        lora_a = _dot(_dot(mix(5), a1_ref[...]).astype(BF16), a2_ref[...])
        proj["a"] = jax.nn.sigmoid(a0_ref[...] + lora_a)
        yield
        proj["r"] = _dot(mix(0), wrkvz_ref[0])
        yield
        proj["v"] = _dot(mix(2), wrkvz_ref[2])
        yield
        proj["z"] = _dot(mix(3), wrkvz_ref[3])

    seg_w = 2 * LANES
    srow = lax.broadcasted_iota(jnp.int32, (seg_w, seg_w), 0)
    scol = lax.broadcasted_iota(jnp.int32, (seg_w, seg_w), 1)
    seg_ones = jnp.where(srow // n == scol // n, 1.0, 0.0).astype(BF16)

    def seg_sum(t, terms):
        pieces = _split(t, terms)
        return jnp.concatenate([sum(_dot(piece[:, j:j + seg_w], seg_ones) for piece in pieces)
                                for j in range(0, e, seg_w)], axis=-1)

    ngroups = RWKV_GROUPS if nb % RWKV_GROUPS == 0 else 1
    gb = nb // ngroups
    grows = gb * tc
    heads = lambda t: jnp.stack([t[(t.shape[0] // gb) * b:(t.shape[0] // gb) * (b + 1), n * h:n * (h + 1)]
                                 for b in range(gb) for h in range(nh)])
    hb = lambda t: heads(t.astype(BF16))
    bdot = lambda spec, lhs, rhs: jnp.einsum(spec, lhs, rhs, preferred_element_type=F32)
    brow = lax.broadcasted_iota(jnp.int32, (grows, grows), 0)
    bcol = lax.broadcasted_iota(jnp.int32, (grows, grows), 1)
    tri = jnp.where((brow >= bcol) & (brow // tc == bcol // tc), 1.0, 0.0).astype(BF16)
    row2 = lax.broadcasted_iota(jnp.int32, (tc, 2 * tc), 0)
    lane2 = lax.broadcasted_iota(jnp.int32, (tc, 2 * tc), 1)
    right = (lane2 >= tc)[None]
    col2 = jnp.where(lane2 >= tc, lane2 - tc, lane2)

    def prepare(gi, ops):
        sl = slice(gi * grows, (gi + 1) * grows)
        kg, ag, lwg = proj["k"][sl], proj["a"][sl], proj["lw"][sl]
        g = sum(_dot(tri, piece) for piece in _split(lwg, 2))
        g_end = g.reshape(gb, tc, e)[:, tc - 1:, :]
        yield
        kmod = kg * (1.0 + (ag - 1.0) * ka_ref[...])
        e_neg = jnp.exp(-g)
        yield
        e_end = jnp.exp(g_end - g.reshape(gb, tc, e)).reshape(grows, e)
        kkr = kg * kk_ref[...]
        kk = kkr / jnp.maximum(jnp.sqrt(seg_sum(kkr * kkr, 2)), 1e-12)
        bvec = kk * ag
        yield
        ops["at"] = hb(-kk * jnp.exp(g - lwg))
        yield
        rg = proj["r"][sl]
        ops["rt"] = hb(rg * jnp.exp(g))
        yield
        ops["btkt"] = jnp.concatenate([hb(bvec * e_neg), hb(kmod * e_neg)], axis=1)
        yield
        ops["bhkh"] = jnp.concatenate([hb(bvec * e_end), hb(kmod * e_end)], axis=1)
        yield
        vg = proj["v"][sl]
        ops.update(v=vg, v_h=hb(vg), decay=heads(jnp.exp(g_end).reshape(gb, e)),
                   bonus=rg * kmod * rk_ref[...])

    def solve(gi, ops):
        v_h = ops["v_h"]
        hs = slice(gi * gb * nh, (gi + 1) * gb * nh)
        s0 = s_ref[hs]
        both = bdot("htk,hsk->hts", jnp.concatenate([ops["at"], ops["rt"]], axis=1),
                    jnp.concatenate([ops["btkt"], s0.astype(BF16)], axis=1))
        aa, from_state = both[:, :, :2 * tc], both[:, :, 2 * tc:]
        yield
        m_a = jnp.where((row2 > col2)[None], aa[:, :tc, :], 0.0).astype(BF16)
        m_r = jnp.where((row2 >= col2)[None], aa[:, tc:, :], 0.0).astype(BF16)
        rhs = from_state[:, :tc] + bdot(
            "hts,hsv->htv", m_a, jnp.concatenate([jnp.zeros_like(v_h), v_h], axis=1))
        yield
        pair = jnp.where(right, jnp.where(row2 == col2, 1.0, 0.0)[None], m_a.astype(F32))
        span = 1
        while span < tc:
            pair_b = pair.astype(BF16)
            pair = bdot("hts,hsu->htu", pair_b[:, :, :tc], pair_b) + jnp.where(right, pair, 0.0)
            span *= 2
            yield
        rhs_b = rhs.astype(BF16)
        sa = bdot("hts,hsv->htv", pair.astype(BF16), jnp.concatenate([jnp.zeros_like(rhs_b), rhs_b], axis=1))
        yield
        sv = jnp.concatenate([sa.astype(BF16), v_h], axis=1)
        out = from_state[:, tc:] + bdot("hts,hsv->htv", m_r, sv)
        s_ref[hs] = s0 * ops["decay"] + bdot("htv,htk->hvk", sv, ops["bhkh"])
        yield
        out = jnp.concatenate(
            [jnp.concatenate([out[b * nh + h] for h in range(nh)], axis=-1) for b in range(gb)], axis=0)
        dev = out - seg_sum(out, 1) * (1.0 / n)
        var = seg_sum(dev * dev, 1) * (1.0 / n)
        ops["res"] = (dev * lax.rsqrt(var + RWKV_GN_EPS) * lgx_ref[...] + lbx_ref[...]
                      + seg_sum(ops["bonus"], 1) * ops["v"])

    def finish(gi, ops):
        sl, bs = slice(gi * grows, (gi + 1) * grows), slice(gi * gb, (gi + 1) * gb)
        y = _dot((ops["res"] * _silu(proj["z"][sl])).astype(BF16), wout_ref[...])
        yield
        pe = _dot(p_ref[0, bs].reshape(grows, p_ref.shape[-1]).astype(BF16), wpe_ref[...])
        o_ref[bs] = _tail(proj["x"][sl], y, pe, wpg_ref, lng_ref, lnb_ref, alpha).reshape(gb, tc, d)

    ops = [dict() for _ in range(ngroups)]
    projecting = project()
    for _ in range(3):
        next(projecting)
    _alternate(projecting, prepare(0, ops[0]))
    for gi in range(ngroups):
        _alternate(solve(gi, ops[gi]),
                  *([prepare(gi + 1, ops[gi + 1])] if gi + 1 < ngroups else []),
                  *([finish(gi - 1, ops[gi - 1])] if gi >= 1 else []))
    _alternate(finish(ngroups - 1, ops[ngroups - 1]))


def _rwkv_layer(x, p, layer, mu, w_rkvz, w0, w1, w2, a0, a1, a2, k_k, k_a, r_k, lnx_g, lnx_b, w_out,
                wpe, wpg, ln_g, ln_b, alpha):
    b, l, d = x.shape
    e = w_out.shape[0]
    nh = e // RWKV_HEAD
    tc = min(RWKV_CHUNK, l)
    nb = RWKV_BATCH if b % RWKV_BATCH == 0 else 1
    row1 = lambda t: t.reshape(1, -1)
    chunk_rows = lambda width: pl.BlockSpec((nb, tc, width), lambda i, t: (i, t, 0))
    return pl.pallas_call(
        functools.partial(_rwkv_layer_kernel, alpha=alpha),
        grid=(b // nb, l // tc),
        in_specs=[chunk_rows(d), pl.BlockSpec((1, nb, tc, p.shape[-1]), lambda i, t: (layer, i, t, 0)),
                  _full(mu.shape), _full(w_rkvz.shape), _full((1, e)), _full(w1.shape), _full(w2.shape),
                  _full((1, e)), _full(a1.shape), _full(a2.shape)] + [_full((1, e))] * 5
                 + [_full(w_out.shape), _full(wpe.shape), _full(wpg.shape), _full((1, d)), _full((1, d))],
        out_specs=chunk_rows(d),
        out_shape=jax.ShapeDtypeStruct((b, l, d), F32),
        scratch_shapes=[pltpu.VMEM((nb * nh, RWKV_HEAD, RWKV_HEAD), F32),
                        pltpu.VMEM((nb, SUBLANES, d), F32)],
        compiler_params=_seq_params(),
        name="rwkv_layer",
    )(x, p, mu, w_rkvz.astype(BF16), row1(w0), w1.astype(BF16), w2.astype(BF16), row1(a0),
      a1.astype(BF16), a2.astype(BF16), row1(k_k), row1(k_a), row1(r_k), row1(lnx_g), row1(lnx_b),
      w_out.astype(BF16), wpe.astype(BF16), wpg.astype(BF16), row1(ln_g), row1(ln_b))


def kernel(x, p, conv_w_in, conv_k, conv_w_out, ssm_w_in, ssm_lam_re, ssm_lam_im, ssm_log_dt, ssm_b_re, ssm_b_im, ssm_c_re, ssm_c_im, ssm_d, ssm_w_glu, ssm_b_glu, ssm_w_out, rwkv_mu, rwkv_w_rkvz, rwkv_w0, rwkv_w1, rwkv_w2, rwkv_a0, rwkv_a1, rwkv_a2, rwkv_k_k, rwkv_k_a, rwkv_r_k, rwkv_lnx_g, rwkv_lnx_b, rwkv_w_out, ple_proj, ple_gate, ln_g, ln_b):
    depth = p.shape[0]
    alpha = (2 * depth) ** 0.25
    for i in range(depth):
        kind, j = i % 3, i // 3
        post = (ple_proj[i], ple_gate[i], ln_g[i], ln_b[i], alpha)
        if kind == 0:
            x = _conv_layer(x, p, i, conv_w_in[j], conv_k[j], conv_w_out[j], *post)
        elif kind == 1:
            x = _s5_layer(x, p, i, ssm_w_in[j], ssm_lam_re[j], ssm_lam_im[j], ssm_log_dt[j],
                          ssm_b_re[j], ssm_b_im[j], ssm_c_re[j], ssm_c_im[j], ssm_d[j],
                          ssm_w_glu[j], ssm_b_glu[j], ssm_w_out[j], *post)
        else:
            x = _rwkv_layer(x, p, i, rwkv_mu[j], rwkv_w_rkvz[j], rwkv_w0[j], rwkv_w1[j], rwkv_w2[j],
                            rwkv_a0[j], rwkv_a1[j], rwkv_a2[j], rwkv_k_k[j], rwkv_k_a[j],
                            rwkv_r_k[j].reshape(-1), rwkv_lnx_g[j], rwkv_lnx_b[j], rwkv_w_out[j], *post)
    return x
```

```python
import functools

import jax
import jax.numpy as jnp
from jax import lax
from jax.experimental import pallas as pl
from jax.experimental.pallas import tpu as pltpu

F32 = jnp.float32
BF16 = jnp.bfloat16

LN_EPS = 1e-5
RWKV_GN_EPS = 64e-5
RWKV_HEAD = 64
SSM_GROUP = 16
SUBLANES = 8
LANES = 128
SLAB_GROUPS = LANES // SSM_GROUP
VMEM_LIMIT = 56 * 1024 * 1024

ROW_TILE = 256
CONV_ROW_TILE = 1024
CONV_SUBTILES = 4
RWKV_CHUNK = 64
RWKV_BATCH = 4
RWKV_GROUPS = 2


def _dot(a, b):
    return jnp.dot(a, b, preferred_element_type=F32)


def _split(t, terms):
    pieces = []
    for i in range(terms):
        pieces.append(t.astype(BF16))
        if i + 1 < terms:
            t = t - pieces[-1].astype(F32)
    return pieces


def _silu(z):
    return z * jax.nn.sigmoid(z)


def _shift_rows(cur, prev, shift):
    rolled = pltpu.roll(cur, shift, 0)
    rows = lax.broadcasted_iota(jnp.int32, (SUBLANES, 1), 0)
    top = jnp.where(rows >= shift, rolled[:SUBLANES], pltpu.roll(prev, shift, 0))
    return jnp.concatenate([top, rolled[SUBLANES:]], axis=0)


def _embed(p_ref, wpe_ref):
    return _dot(p_ref[0, 0].astype(BF16), wpe_ref[...])


def _tail(x, y, pe, wpg_ref, lng_ref, lnb_ref, alpha):
    r = alpha * x + y
    r = r + pe * jax.nn.sigmoid(_dot(r.astype(BF16), wpg_ref[...]))
    d = r - jnp.mean(r, axis=-1, keepdims=True)
    var = jnp.mean(d * d, axis=-1, keepdims=True)
    return d * lax.rsqrt(var + LN_EPS) * lng_ref[...] + lnb_ref[...]


def _conv_layer_kernel(x_ref, p_ref, win_ref, ck_ref, wout_ref, wpe_ref, wpg_ref, lng_ref, lnb_ref,
                       o_ref, carry_ref, *, alpha):
    e = wout_ref.shape[0]
    sub = x_ref.shape[1] // CONV_SUBTILES

    @pl.when(pl.program_id(1) == 0)
    def _():
        carry_ref[...] = jnp.zeros_like(carry_ref)

    def mix(i, prev):
        x = x_ref[0, i * sub:(i + 1) * sub, :]
        xb = x.astype(BF16)
        proj = lambda c: _dot(xb, win_ref[:, c * e:(c + 1) * e])
        u = proj(1) * proj(2)
        conv = (ck_ref[0:1, :] * _shift_rows(u, prev, 2) + ck_ref[1:2, :] * _shift_rows(u, prev, 1)
                + ck_ref[2:3, :] * u)
        return x, proj(0) * conv * _silu(proj(3)), u[sub - SUBLANES:, :]

    def finish(i, x, g):
        y = _dot(g.astype(BF16), wout_ref[...])
        pe = _dot(p_ref[0, 0, i * sub:(i + 1) * sub, :].astype(BF16), wpe_ref[...])
        o_ref[0, i * sub:(i + 1) * sub, :] = _tail(x, y, pe, wpg_ref, lng_ref, lnb_ref, alpha)

    prev, pending = carry_ref[...], None
    for i in range(CONV_SUBTILES):
        x, g, prev = mix(i, prev)
        if pending is not None:
            finish(*pending)
        pending = (i, x, g)
    finish(*pending)
    carry_ref[...] = prev


def _full(shape):
    return pl.BlockSpec(shape, lambda b, t: (0,) * len(shape))


def _rows(tm, width):
    return pl.BlockSpec((1, tm, width), lambda b, t: (b, t, 0))


def _layer_rows(layer, tm, width):
    return pl.BlockSpec((1, 1, tm, width), lambda b, t: (layer, b, t, 0))


def _seq_params():
    return pltpu.CompilerParams(dimension_semantics=("arbitrary", "arbitrary"),
                                vmem_limit_bytes=VMEM_LIMIT)


def _conv_layer(x, p, layer, w_in, conv_k, w_out, wpe, wpg, ln_g, ln_b, alpha):
    b, l, d = x.shape
    e = w_out.shape[0]
    tm = min(CONV_ROW_TILE, l)
    return pl.pallas_call(
        functools.partial(_conv_layer_kernel, alpha=alpha),
        grid=(b, l // tm),
        in_specs=[_rows(tm, d), _layer_rows(layer, tm, p.shape[-1]), _full(w_in.shape), _full(conv_k.shape),
                  _full(w_out.shape), _full(wpe.shape), _full(wpg.shape), _full((1, d)), _full((1, d))],
        out_specs=_rows(tm, d),
        out_shape=jax.ShapeDtypeStruct((b, l, d), F32),
        scratch_shapes=[pltpu.VMEM((SUBLANES, e), F32)],
        compiler_params=_seq_params(),
        name="conv_layer",
    )(x, p, w_in.astype(BF16), conv_k, w_out.astype(BF16), wpe.astype(BF16), wpg.astype(BF16),
      ln_g.reshape(1, d), ln_b.reshape(1, d))


def _s5_discretise_kernel(lre_ref, lim_ref, ldt_ref, bre_ref, bim_ref,
                          are_ref, aim_ref, pre_ref, pim_ref, bbre_ref, bbim_ref, *, seg_len):
    lre, lim = lre_ref[...], lim_ref[...]
    dt = jnp.exp(ldt_ref[...])
    mag = jnp.exp(lre * dt)
    are, aim = mag * jnp.cos(lim * dt), mag * jnp.sin(lim * dt)
    den = lre * lre + lim * lim
    cre = ((are - 1.0) * lre + aim * lim) / den
    cim = (aim * lre - (are - 1.0) * lim) / den
    bre, bim = bre_ref[...], bim_ref[...]
    bbre_ref[...] = cre[:, None, :] * bre - cim[:, None, :] * bim
    bbim_ref[...] = cre[:, None, :] * bim + cim[:, None, :] * bre
    are_ref[...] = are
    aim_ref[...] = aim
    sre, sim = are, aim
    for _ in range(seg_len.bit_length() - 1):
        sre, sim = sre * sre - sim * sim, 2.0 * sre * sim
    pr, pi = sre, sim
    for n in range(SUBLANES):
        pre_ref[n] = pr
        pim_ref[n] = pi
        pr, pi = pr * sre - pi * sim, pr * sim + pi * sre


def _s5_layer_kernel(x_ref, p_ref, win_ref, wbre_ref, wbim_ref, wcre_ref, wcim_ref, tab_ref, dsk_ref,
                     wglu_ref, bglu_ref, wout_ref, wpe_ref, wpg_ref, lng_ref, lnb_ref,
                     o_ref, state_ref, sre_ref, sim_ref, y_ref, *, alpha):
    e = wout_ref.shape[0]
    tm = x_ref.shape[1]
    nslab = wbre_ref.shape[0]
    w = wbre_ref.shape[2]
    seg_len = tm // SUBLANES

    @pl.when(pl.program_id(1) == 0)
    def _():
        state_ref[...] = jnp.zeros_like(state_ref)

    x = x_ref[0]
    xb = x.astype(BF16)
    u = _dot(xb, win_ref[:, :e])

    new_r = lax.broadcasted_iota(jnp.int32, (tm, tm), 0)
    old_r = lax.broadcasted_iota(jnp.int32, (tm, tm), 1)
    regroup = jnp.where(old_r == (new_r % SUBLANES) * seg_len + new_r // SUBLANES, 1.0, 0.0).astype(BF16)
    ungroup = jnp.where(new_r == (old_r % SUBLANES) * seg_len + old_r // SUBLANES, 1.0, 0.0).astype(BF16)
    ub = _dot(regroup, u.astype(BF16)).astype(BF16)
    rows = lax.broadcasted_iota(jnp.int32, (SUBLANES, 1), 0)

    def project_in(s):
        us = ub[:, s * LANES:(s + 1) * LANES]
        sre_ref[s] = _dot(us, wbre_ref[s])
        sim_ref[s] = _dot(us, wbim_ref[s])

    z_parts = []
    z_w = 2 * e // nslab
    project_in(0)
    for s in range(nslab):
        if s + 1 < nslab:
            project_in(s + 1)
        if s % 2 == 0:
            z_parts.append(_dot(xb, win_ref[:, e + (s // 2) * z_w:e + (s // 2 + 1) * z_w]))
        if s == 1:
            pe = _embed(p_ref, wpe_ref)
        sre, sim = sre_ref.at[s], sim_ref.at[s]
        ar, ai = tab_ref[s, 0], tab_ref[s, 1]

        def local_final(i, c):
            cr, ci = c
            r0 = pl.multiple_of(i * SUBLANES, SUBLANES)
            return (cr * ar - ci * ai + sre[pl.ds(r0, SUBLANES), :],
                    cr * ai + ci * ar + sim[pl.ds(r0, SUBLANES), :])

        zero = jnp.zeros((SUBLANES, w), F32)
        fr, fi = lax.fori_loop(0, seg_len, local_final, (zero, zero), unroll=True)
        for lvl, dist in enumerate((1, 2, 4)):
            pr, pi = tab_ref[s, 2 + 2 * lvl], tab_ref[s, 3 + 2 * lvl]
            gr, gi = pltpu.roll(fr, dist, 0), pltpu.roll(fi, dist, 0)
            fr, fi = fr + pr * gr - pi * gi, fi + pr * gi + pi * gr
        cr0, ci0 = state_ref[2 * s:2 * s + 1, :], state_ref[2 * s + 1:2 * s + 2, :]
        qr, qi = tab_ref[s, 8], tab_ref[s, 9]
        init_r = qr * cr0 - qi * ci0 + jnp.where(rows >= 1, pltpu.roll(fr, 1, 0), 0.0)
        init_i = qr * ci0 + qi * cr0 + jnp.where(rows >= 1, pltpu.roll(fi, 1, 0), 0.0)

        def all_states(i, c):
            cr, ci = c
            r0 = pl.multiple_of(i * SUBLANES, SUBLANES)
            nr = cr * ar - ci * ai + sre[pl.ds(r0, SUBLANES), :]
            ni = cr * ai + ci * ar + sim[pl.ds(r0, SUBLANES), :]
            sre[pl.ds(r0, SUBLANES), :] = nr
            sim[pl.ds(r0, SUBLANES), :] = ni
            return nr, ni

        lr, li = lax.fori_loop(0, seg_len, all_states, (init_r, init_i), unroll=True)
        state_ref[2 * s:2 * s + 1, :] = lr[SUBLANES - 1:, :]
        state_ref[2 * s + 1:2 * s + 2, :] = li[SUBLANES - 1:, :]
        y_ref[:, s * LANES:(s + 1) * LANES] = (_dot(sre[...].astype(BF16), wcre_ref[s])
                                               + _dot(sim[...].astype(BF16), wcim_ref[s]))

    y = sum(_dot(ungroup, piece) for piece in _split(y_ref[...], 2))

    y = jax.nn.gelu(y + dsk_ref[...] * u)
    y = y * jax.nn.sigmoid(_dot(y.astype(BF16), wglu_ref[...]) + bglu_ref[...])
    z = jnp.concatenate(z_parts, axis=-1)
    out = _dot((y * _silu(z)).astype(BF16), wout_ref[...])
    o_ref[0] = _tail(x, out, pe, wpg_ref, lng_ref, lnb_ref, alpha)


def _s5_layer(x, p, layer, w_in, lam_re, lam_im, log_dt, b_re, b_im, c_re, c_im, d_skip, w_glu, b_glu, w_out,
              wpe, wpg, ln_g, ln_b, alpha):
    b, l, d = x.shape
    e = w_out.shape[0]
    g, pst = lam_re.shape
    hch = b_re.shape[-1]
    nslab = g // SLAB_GROUPS
    w = SLAB_GROUPS * pst
    tm = min(ROW_TILE, l)
    seg_len = tm // SUBLANES
    assert seg_len & (seg_len - 1) == 0, "segment length must be a power of two"

    gp1 = jax.ShapeDtypeStruct((g, pst), F32)
    gp = jax.ShapeDtypeStruct((SUBLANES, g, pst), F32)
    gb = jax.ShapeDtypeStruct((g, hch, pst), F32)
    a_re, a_im, pw_re, pw_im, bb_re, bb_im = pl.pallas_call(
        functools.partial(_s5_discretise_kernel, seg_len=seg_len),
        out_shape=(gp1, gp1, gp, gp, gb, gb), name="s5_discretise",
    )(lam_re, lam_im, log_dt.reshape(g, 1), jnp.swapaxes(b_re, 1, 2), jnp.swapaxes(b_im, 1, 2))

    eye = jnp.eye(SLAB_GROUPS, dtype=F32)
    blk_in = lambda m: jnp.einsum("sghp,gk->sghkp", m.reshape(nslab, SLAB_GROUPS, hch, pst),
                                  eye).reshape(nslab, LANES, w).astype(BF16)
    blk_out = lambda m: jnp.einsum("sghp,gk->skpgh", m.reshape(nslab, SLAB_GROUPS, hch, pst),
                                   eye).reshape(nslab, w, LANES).astype(BF16)
    rows = jnp.arange(SUBLANES)[:, None, None]
    flat = lambda m: m.reshape(SUBLANES, nslab, w)
    tabs = [jnp.broadcast_to(m.reshape(1, nslab, w), (SUBLANES, nslab, w)) for m in (a_re, a_im)]
    for dist in (1, 2, 4):
        tabs += [jnp.where(rows >= dist, flat(m)[dist - 1][None], 0.0) for m in (pw_re, pw_im)]
    tabs.append(jnp.concatenate([jnp.ones((1, nslab, w), F32), flat(pw_re)[:SUBLANES - 1]]))
    tabs.append(jnp.concatenate([jnp.zeros((1, nslab, w), F32), flat(pw_im)[:SUBLANES - 1]]))
    tab = jnp.transpose(jnp.stack(tabs), (2, 0, 1, 3))

    return pl.pallas_call(
        functools.partial(_s5_layer_kernel, alpha=alpha),
        grid=(b, l // tm),
        in_specs=[_rows(tm, d), _layer_rows(layer, tm, p.shape[-1]), _full(w_in.shape),
                  _full((nslab, LANES, w)), _full((nslab, LANES, w)),
                  _full((nslab, w, LANES)), _full((nslab, w, LANES)),
                  _full(tab.shape), _full((1, e)),
                  _full(w_glu.shape), _full((1, e)), _full(w_out.shape), _full(wpe.shape),
                  _full(wpg.shape), _full((1, d)), _full((1, d))],
        out_specs=_rows(tm, d),
        out_shape=jax.ShapeDtypeStruct((b, l, d), F32),
        scratch_shapes=[pltpu.VMEM((2 * nslab, w), F32), pltpu.VMEM((nslab, tm, w), F32),
                        pltpu.VMEM((nslab, tm, w), F32), pltpu.VMEM((tm, e), F32)],
        compiler_params=_seq_params(),
        name="s5_layer",
    )(x, p, w_in.astype(BF16), blk_in(bb_re), blk_in(bb_im),
      blk_out(c_re), blk_out(-c_im),
      tab, d_skip.reshape(1, e), w_glu.astype(BF16), b_glu.reshape(1, e), w_out.astype(BF16),
      wpe.astype(BF16), wpg.astype(BF16), ln_g.reshape(1, d), ln_b.reshape(1, d))


def _rwkv_layer_kernel(x_ref, p_ref, mu_ref, wrkvz_ref, w0_ref, w1_ref, w2_ref, a0_ref, a1_ref, a2_ref,
                       kk_ref, ka_ref, rk_ref, lgx_ref, lbx_ref, wout_ref, wpe_ref, wpg_ref, lng_ref, lnb_ref,
                       o_ref, s_ref, carry_ref, *, alpha):
    nb, tc, d = x_ref.shape
    e = wout_ref.shape[0]
    n = s_ref.shape[1]
    nh = s_ref.shape[0] // nb
    rows = nb * tc

    @pl.when(pl.program_id(1) == 0)
    def _():
        s_ref[...] = jnp.zeros_like(s_ref)
        carry_ref[...] = jnp.zeros_like(carry_ref)

    proj = {}

    def project():
        x3 = x_ref[...]
        x = x3.reshape(rows, d)
        shifted = jnp.concatenate([_shift_rows(x3[b], carry_ref[b], 1) for b in range(nb)], axis=0)
        carry_ref[...] = x3[:, tc - SUBLANES:, :]
        dx = shifted - x
        mix = lambda i: (x + dx * mu_ref[i:i + 1, :]).astype(BF16)
        lora_w = _dot(jnp.tanh(_dot(mix(4), w1_ref[...])).astype(BF16), w2_ref[...])
        w_log = -jax.nn.softplus(-(w0_ref[...] + lora_w)) - 0.5
        proj.update(x=x, lw=-jnp.exp(w_log))
        yield
        proj["k"] = _dot(mix(1), wrkvz_ref[1])
        yield
        lora_a = _dot(_dot(mix(5), a1_ref[...]).astype(BF16), a2_ref[...])
        proj["a"] = jax.nn.sigmoid(a0_ref[...] + lora_a)
        yield
        proj["r"] = _dot(mix(0), wrkvz_ref[0])
        yield
        proj["v"] = _dot(mix(2), wrkvz_ref[2])
        yield
        proj["z"] = _dot(mix(3), wrkvz_ref[3])

    seg_w = 2 * LANES
    srow = lax.broadcasted_iota(jnp.int32, (seg_w, seg_w), 0)
    scol = lax.broadcasted_iota(jnp.int32, (seg_w, seg_w), 1)
    seg_ones = jnp.where(srow // n == scol // n, 1.0, 0.0).astype(BF16)

    def seg_sum(t, terms):
        pieces = _split(t, terms)
        return jnp.concatenate([sum(_dot(piece[:, j:j + seg_w], seg_ones) for piece in pieces)
                                for j in range(0, e, seg_w)], axis=-1)

    ngroups = RWKV_GROUPS if nb % RWKV_GROUPS == 0 else 1
    gb = nb // ngroups
    grows = gb * tc
    heads = lambda t: jnp.stack([t[(t.shape[0] // gb) * b:(t.shape[0] // gb) * (b + 1), n * h:n * (h + 1)]
                                 for b in range(gb) for h in range(nh)])
    hb = lambda t: heads(t.astype(BF16))
    bdot = lambda spec, lhs, rhs: jnp.einsum(spec, lhs, rhs, preferred_element_type=F32)
    brow = lax.broadcasted_iota(jnp.int32, (grows, grows), 0)
    bcol = lax.broadcasted_iota(jnp.int32, (grows, grows), 1)
    tri = jnp.where((brow >= bcol) & (brow // tc == bcol // tc), 1.0, 0.0).astype(BF16)
    row2 = lax.broadcasted_iota(jnp.int32, (tc, 2 * tc), 0)
    lane2 = lax.broadcasted_iota(jnp.int32, (tc, 2 * tc), 1)
    right = (lane2 >= tc)[None]
    col2 = jnp.where(lane2 >= tc, lane2 - tc, lane2)

    def prepare(gi, ops):
        sl = slice(gi * grows, (gi + 1) * grows)
        kg, ag, lwg = proj["k"][sl], proj["a"][sl], proj["lw"][sl]
        g = sum(_dot(tri, piece) for piece in _split(lwg, 2))
        g_end = g.reshape(gb, tc, e)[:, tc - 1:, :]
        yield
        kmod = kg * (1.0 + (ag - 1.0) * ka_ref[...])
        e_neg = jnp.exp(-g)
        yield
        e_end = jnp.exp(g_end - g.reshape(gb, tc, e)).reshape(grows, e)
        kkr = kg * kk_ref[...]
        kk = kkr / jnp.maximum(jnp.sqrt(seg_sum(kkr * kkr, 2)), 1e-12)
        bvec = kk * ag
        yield
        ops["at"] = hb(-kk * jnp.exp(g - lwg))
        yield
        rg = proj["r"][sl]
        ops["rt"] = hb(rg * jnp.exp(g))
        yield
        ops["btkt"] = jnp.concatenate([hb(bvec * e_neg), hb(kmod * e_neg)], axis=1)
        yield
        ops["bhkh"] = jnp.concatenate([hb(bvec * e_end), hb(kmod * e_end)], axis=1)
        yield
        vg = proj["v"][sl]
        ops.update(v=vg, v_h=hb(vg), decay=heads(jnp.exp(g_end).reshape(gb, e)),
                   bonus=rg * kmod * rk_ref[...])

    def solve(gi, ops):
        v_h = ops["v_h"]
        hs = slice(gi * gb * nh, (gi + 1) * gb * nh)
        s0 = s_ref[hs]
        both = bdot("htk,hsk->hts", jnp.concatenate([ops["at"], ops["rt"]], axis=1),
                    jnp.concatenate([ops["btkt"], s0.astype(BF16)], axis=1))
        aa, from_state = both[:, :, :2 * tc], both[:, :, 2 * tc:]
        yield
        m_a = jnp.where((row2 > col2)[None], aa[:, :tc, :], 0.0).astype(BF16)
        m_r = jnp.where((row2 >= col2)[None], aa[:, tc:, :], 0.0).astype(BF16)
        rhs = from_state[:, :tc] + bdot(
            "hts,hsv->htv", m_a, jnp.concatenate([jnp.zeros_like(v_h), v_h], axis=1))
        yield
        pair = jnp.where(right, jnp.where(row2 == col2, 1.0, 0.0)[None], m_a.astype(F32))
        span = 1
        while span < tc:
            pair_b = pair.astype(BF16)
            pair = bdot("hts,hsu->htu", pair_b[:, :, :tc], pair_b) + jnp.where(right, pair, 0.0)
            span *= 2
            yield
        rhs_b = rhs.astype(BF16)
        sa = bdot("hts,hsv->htv", pair.astype(BF16), jnp.concatenate([jnp.zeros_like(rhs_b), rhs_b], axis=1))
        yield
        sv = jnp.concatenate([sa.astype(BF16), v_h], axis=1)
        out = from_state[:, tc:] + bdot("hts,hsv->htv", m_r, sv)
        s_ref[hs] = s0 * ops["decay"] + bdot("htv,htk->hvk", sv, ops["bhkh"])
        yield
        out = jnp.concatenate(
            [jnp.concatenate([out[b * nh + h] for h in range(nh)], axis=-1) for b in range(gb)], axis=0)
        dev = out - seg_sum(out, 1) * (1.0 / n)
        var = seg_sum(dev * dev, 1) * (1.0 / n)
        ops["res"] = (dev * lax.rsqrt(var + RWKV_GN_EPS) * lgx_ref[...] + lbx_ref[...]
                      + seg_sum(ops["bonus"], 1) * ops["v"])

    def finish(gi, ops):
        sl, bs = slice(gi * grows, (gi + 1) * grows), slice(gi * gb, (gi + 1) * gb)
        y = _dot((ops["res"] * _silu(proj["z"][sl])).astype(BF16), wout_ref[...])
        yield
        pe = _dot(p_ref[0, bs].reshape(grows, p_ref.shape[-1]).astype(BF16), wpe_ref[...])
        o_ref[bs] = _tail(proj["x"][sl], y, pe, wpg_ref, lng_ref, lnb_ref, alpha).reshape(gb, tc, d)

    def alternate(*stages):
        stages = list(stages)
        while stages:
            for gen in list(stages):
                if next(gen, stages) is stages:
                    stages.remove(gen)

    ops = [dict() for _ in range(ngroups)]
    projecting = project()
    for _ in range(3):
        next(projecting)
    alternate(projecting, prepare(0, ops[0]))
    for gi in range(ngroups):
        alternate(solve(gi, ops[gi]),
                  *([prepare(gi + 1, ops[gi + 1])] if gi + 1 < ngroups else []),
                  *([finish(gi - 1, ops[gi - 1])] if gi >= 1 else []))
    alternate(finish(ngroups - 1, ops[ngroups - 1]))


def _rwkv_layer(x, p, layer, mu, w_rkvz, w0, w1, w2, a0, a1, a2, k_k, k_a, r_k, lnx_g, lnx_b, w_out,
                wpe, wpg, ln_g, ln_b, alpha):
    b, l, d = x.shape
    e = w_out.shape[0]
    nh = e // RWKV_HEAD
    tc = min(RWKV_CHUNK, l)
    nb = RWKV_BATCH if b % RWKV_BATCH == 0 else 1
    row1 = lambda t: t.reshape(1, -1)
    chunk_rows = lambda width: pl.BlockSpec((nb, tc, width), lambda i, t: (i, t, 0))
    return pl.pallas_call(
        functools.partial(_rwkv_layer_kernel, alpha=alpha),
        grid=(b // nb, l // tc),
        in_specs=[chunk_rows(d), pl.BlockSpec((1, nb, tc, p.shape[-1]), lambda i, t: (layer, i, t, 0)),
                  _full(mu.shape), _full(w_rkvz.shape), _full((1, e)), _full(w1.shape), _full(w2.shape),
                  _full((1, e)), _full(a1.shape), _full(a2.shape)] + [_full((1, e))] * 5
                 + [_full(w_out.shape), _full(wpe.shape), _full(wpg.shape), _full((1, d)), _full((1, d))],
        out_specs=chunk_rows(d),
        out_shape=jax.ShapeDtypeStruct((b, l, d), F32),
        scratch_shapes=[pltpu.VMEM((nb * nh, RWKV_HEAD, RWKV_HEAD), F32),
                        pltpu.VMEM((nb, SUBLANES, d), F32)],
        compiler_params=_seq_params(),
        name="rwkv_layer",
    )(x, p, mu, w_rkvz.astype(BF16), row1(w0), w1.astype(BF16), w2.astype(BF16), row1(a0),
      a1.astype(BF16), a2.astype(BF16), row1(k_k), row1(k_a), row1(r_k), row1(lnx_g), row1(lnx_b),
      w_out.astype(BF16), wpe.astype(BF16), wpg.astype(BF16), row1(ln_g), row1(ln_b))


def kernel(x, p, conv_w_in, conv_k, conv_w_out, ssm_w_in, ssm_lam_re, ssm_lam_im, ssm_log_dt, ssm_b_re, ssm_b_im, ssm_c_re, ssm_c_im, ssm_d, ssm_w_glu, ssm_b_glu, ssm_w_out, rwkv_mu, rwkv_w_rkvz, rwkv_w0, rwkv_w1, rwkv_w2, rwkv_a0, rwkv_a1, rwkv_a2, rwkv_k_k, rwkv_k_a, rwkv_r_k, rwkv_lnx_g, rwkv_lnx_b, rwkv_w_out, ple_proj, ple_gate, ln_g, ln_b):
    depth = p.shape[0]
    alpha = (2 * depth) ** 0.25
    for i in range(depth):
        kind, j = i % 3, i // 3
        post = (ple_proj[i], ple_gate[i], ln_g[i], ln_b[i], alpha)
        if kind == 0:
            x = _conv_layer(x, p, i, conv_w_in[j], conv_k[j], conv_w_out[j], *post)
        elif kind == 1:
            x = _s5_layer(x, p, i, ssm_w_in[j], ssm_lam_re[j], ssm_lam_im[j], ssm_log_dt[j],
                          ssm_b_re[j], ssm_b_im[j], ssm_c_re[j], ssm_c_im[j], ssm_d[j],
                          ssm_w_glu[j], ssm_b_glu[j], ssm_w_out[j], *post)
        else:
            x = _rwkv_layer(x, p, i, rwkv_mu[j], rwkv_w_rkvz[j], rwkv_w0[j], rwkv_w1[j], rwkv_w2[j],
                            rwkv_a0[j], rwkv_a1[j], rwkv_a2[j], rwkv_k_k[j], rwkv_k_a[j],
                            rwkv_r_k[j].reshape(-1), rwkv_lnx_g[j], rwkv_lnx_b[j], rwkv_w_out[j], *post)
    return x
```

```python
import functools

import jax
import jax.numpy as jnp
from jax import lax
from jax.experimental import pallas as pl
from jax.experimental.pallas import tpu as pltpu

F32 = jnp.float32
BF16 = jnp.bfloat16

LN_EPS = 1e-5
RWKV_GN_EPS = 64e-5
RWKV_HEAD = 64
SSM_GROUP = 16
SUBLANES = 8
LANES = 128
SLAB_GROUPS = LANES // SSM_GROUP
VMEM_LIMIT = 56 * 1024 * 1024

ROW_TILE = 256
CONV_ROW_TILE = 1024
CONV_SUBTILES = 4
RWKV_CHUNK = 64
RWKV_BATCH = 4
RWKV_GROUPS = 2


def _dot(a, b):
    return jnp.dot(a, b, preferred_element_type=F32)


def _split(t, terms):
    pieces = []
    for i in range(terms):
        pieces.append(t.astype(BF16))
        if i + 1 < terms:
            t = t - pieces[-1].astype(F32)
    return pieces


def _silu(z):
    return z * jax.nn.sigmoid(z)


def _shift_rows(cur, prev, shift):
    rolled = pltpu.roll(cur, shift, 0)
    rows = lax.broadcasted_iota(jnp.int32, (SUBLANES, 1), 0)
    top = jnp.where(rows >= shift, rolled[:SUBLANES], pltpu.roll(prev, shift, 0))
    return jnp.concatenate([top, rolled[SUBLANES:]], axis=0)


def _embed(p_ref, wpe_ref):
    return _dot(p_ref[0, 0].astype(BF16), wpe_ref[...])


def _tail(x, y, pe, wpg_ref, lng_ref, lnb_ref, alpha):
    r = alpha * x + y
    r = r + pe * jax.nn.sigmoid(_dot(r.astype(BF16), wpg_ref[...]))
    d = r - jnp.mean(r, axis=-1, keepdims=True)
    var = jnp.mean(d * d, axis=-1, keepdims=True)
    return d * lax.rsqrt(var + LN_EPS) * lng_ref[...] + lnb_ref[...]


def _conv_layer_kernel(x_ref, p_ref, win_ref, ck_ref, wout_ref, wpe_ref, wpg_ref, lng_ref, lnb_ref,
                       o_ref, carry_ref, *, alpha):
    e = wout_ref.shape[0]
    sub = x_ref.shape[1] // CONV_SUBTILES

    @pl.when(pl.program_id(1) == 0)
    def _():
        carry_ref[...] = jnp.zeros_like(carry_ref)

    def mix(i, prev):
        x = x_ref[0, i * sub:(i + 1) * sub, :]
        xb = x.astype(BF16)
        proj = lambda c: _dot(xb, win_ref[:, c * e:(c + 1) * e])
        u = proj(1) * proj(2)
        conv = (ck_ref[0:1, :] * _shift_rows(u, prev, 2) + ck_ref[1:2, :] * _shift_rows(u, prev, 1)
                + ck_ref[2:3, :] * u)
        return x, proj(0) * conv * _silu(proj(3)), u[sub - SUBLANES:, :]

    def finish(i, x, g):
        y = _dot(g.astype(BF16), wout_ref[...])
        pe = _dot(p_ref[0, 0, i * sub:(i + 1) * sub, :].astype(BF16), wpe_ref[...])
        o_ref[0, i * sub:(i + 1) * sub, :] = _tail(x, y, pe, wpg_ref, lng_ref, lnb_ref, alpha)

    prev, pending = carry_ref[...], None
    for i in range(CONV_SUBTILES):
        x, g, prev = mix(i, prev)
        if pending is not None:
            finish(*pending)
        pending = (i, x, g)
    finish(*pending)
    carry_ref[...] = prev


def _full(shape):
    return pl.BlockSpec(shape, lambda b, t: (0,) * len(shape))


def _rows(tm, width):
    return pl.BlockSpec((1, tm, width), lambda b, t: (b, t, 0))


def _layer_rows(layer, tm, width):
    return pl.BlockSpec((1, 1, tm, width), lambda b, t: (layer, b, t, 0))


def _seq_params():
    return pltpu.CompilerParams(dimension_semantics=("arbitrary", "arbitrary"),
                                vmem_limit_bytes=VMEM_LIMIT)


def _conv_layer(x, p, layer, w_in, conv_k, w_out, wpe, wpg, ln_g, ln_b, alpha):
    b, l, d = x.shape
    e = w_out.shape[0]
    tm = min(CONV_ROW_TILE, l)
    return pl.pallas_call(
        functools.partial(_conv_layer_kernel, alpha=alpha),
        grid=(b, l // tm),
        in_specs=[_rows(tm, d), _layer_rows(layer, tm, p.shape[-1]), _full(w_in.shape), _full(conv_k.shape),
                  _full(w_out.shape), _full(wpe.shape), _full(wpg.shape), _full((1, d)), _full((1, d))],
        out_specs=_rows(tm, d),
        out_shape=jax.ShapeDtypeStruct((b, l, d), F32),
        scratch_shapes=[pltpu.VMEM((SUBLANES, e), F32)],
        compiler_params=_seq_params(),
        name="conv_layer",
    )(x, p, w_in.astype(BF16), conv_k, w_out.astype(BF16), wpe.astype(BF16), wpg.astype(BF16),
      ln_g.reshape(1, d), ln_b.reshape(1, d))


def _s5_discretise_kernel(lre_ref, lim_ref, ldt_ref, bre_ref, bim_ref,
                          are_ref, aim_ref, pre_ref, pim_ref, bbre_ref, bbim_ref, *, seg_len):
    lre, lim = lre_ref[...], lim_ref[...]
    dt = jnp.exp(ldt_ref[...])
    mag = jnp.exp(lre * dt)
    are, aim = mag * jnp.cos(lim * dt), mag * jnp.sin(lim * dt)
    den = lre * lre + lim * lim
    cre = ((are - 1.0) * lre + aim * lim) / den
    cim = (aim * lre - (are - 1.0) * lim) / den
    bre, bim = bre_ref[...], bim_ref[...]
    bbre_ref[...] = cre[:, None, :] * bre - cim[:, None, :] * bim
    bbim_ref[...] = cre[:, None, :] * bim + cim[:, None, :] * bre
    are_ref[...] = are
    aim_ref[...] = aim
    sre, sim = are, aim
    for _ in range(seg_len.bit_length() - 1):
        sre, sim = sre * sre - sim * sim, 2.0 * sre * sim
    pr, pi = sre, sim
    for n in range(SUBLANES):
        pre_ref[n] = pr
        pim_ref[n] = pi
        pr, pi = pr * sre - pi * sim, pr * sim + pi * sre


def _s5_layer_kernel(x_ref, p_ref, win_ref, wbre_ref, wbim_ref, wcre_ref, wcim_ref, tab_ref, dsk_ref,
                     wglu_ref, bglu_ref, wout_ref, wpe_ref, wpg_ref, lng_ref, lnb_ref,
                     o_ref, state_ref, sre_ref, sim_ref, y_ref, *, alpha):
    e = wout_ref.shape[0]
    tm = x_ref.shape[1]
    nslab = wbre_ref.shape[0]
    w = wbre_ref.shape[2]
    seg_len = tm // SUBLANES

    @pl.when(pl.program_id(1) == 0)
    def _():
        state_ref[...] = jnp.zeros_like(state_ref)

    x = x_ref[0]
    xb = x.astype(BF16)
    u = _dot(xb, win_ref[:, :e])

    new_r = lax.broadcasted_iota(jnp.int32, (tm, tm), 0)
    old_r = lax.broadcasted_iota(jnp.int32, (tm, tm), 1)
    regroup = jnp.where(old_r == (new_r % SUBLANES) * seg_len + new_r // SUBLANES, 1.0, 0.0).astype(BF16)
    ungroup = jnp.where(new_r == (old_r % SUBLANES) * seg_len + old_r // SUBLANES, 1.0, 0.0).astype(BF16)
    ub = _dot(regroup, u.astype(BF16)).astype(BF16)
    rows = lax.broadcasted_iota(jnp.int32, (SUBLANES, 1), 0)

    def project_in(s):
        us = ub[:, s * LANES:(s + 1) * LANES]
        sre_ref[s] = _dot(us, wbre_ref[s])
        sim_ref[s] = _dot(us, wbim_ref[s])

    z_parts = []
    z_w = 2 * e // nslab
    project_in(0)
    for s in range(nslab):
        if s + 1 < nslab:
            project_in(s + 1)
        if s < nslab // 2:
            z_parts.append(_dot(xb, win_ref[:, e + s * z_w:e + (s + 1) * z_w]))
        if s == nslab // 2:
            pe = _embed(p_ref, wpe_ref)
        sre, sim = sre_ref.at[s], sim_ref.at[s]
        ar, ai = tab_ref[s, 0], tab_ref[s, 1]

        def local_final(i, c):
            cr, ci = c
            r0 = pl.multiple_of(i * SUBLANES, SUBLANES)
            return (cr * ar - ci * ai + sre[pl.ds(r0, SUBLANES), :],
                    cr * ai + ci * ar + sim[pl.ds(r0, SUBLANES), :])

        zero = jnp.zeros((SUBLANES, w), F32)
        fr, fi = lax.fori_loop(0, seg_len, local_final, (zero, zero), unroll=True)
        for lvl, dist in enumerate((1, 2, 4)):
            pr, pi = tab_ref[s, 2 + 2 * lvl], tab_ref[s, 3 + 2 * lvl]
            gr, gi = pltpu.roll(fr, dist, 0), pltpu.roll(fi, dist, 0)
            fr, fi = fr + pr * gr - pi * gi, fi + pr * gi + pi * gr
        cr0, ci0 = state_ref[2 * s:2 * s + 1, :], state_ref[2 * s + 1:2 * s + 2, :]
        qr, qi = tab_ref[s, 8], tab_ref[s, 9]
        init_r = qr * cr0 - qi * ci0 + jnp.where(rows >= 1, pltpu.roll(fr, 1, 0), 0.0)
        init_i = qr * ci0 + qi * cr0 + jnp.where(rows >= 1, pltpu.roll(fi, 1, 0), 0.0)

        def all_states(i, c):
            cr, ci = c
            r0 = pl.multiple_of(i * SUBLANES, SUBLANES)
            nr = cr * ar - ci * ai + sre[pl.ds(r0, SUBLANES), :]
            ni = cr * ai + ci * ar + sim[pl.ds(r0, SUBLANES), :]
            sre[pl.ds(r0, SUBLANES), :] = nr
            sim[pl.ds(r0, SUBLANES), :] = ni
            return nr, ni

        lr, li = lax.fori_loop(0, seg_len, all_states, (init_r, init_i), unroll=True)
        state_ref[2 * s:2 * s + 1, :] = lr[SUBLANES - 1:, :]
        state_ref[2 * s + 1:2 * s + 2, :] = li[SUBLANES - 1:, :]
        y_ref[:, s * LANES:(s + 1) * LANES] = (_dot(sre[...].astype(BF16), wcre_ref[s])
                                               + _dot(sim[...].astype(BF16), wcim_ref[s]))

    y = sum(_dot(ungroup, piece) for piece in _split(y_ref[...], 2))

    y = jax.nn.gelu(y + dsk_ref[...] * u)
    y = y * jax.nn.sigmoid(_dot(y.astype(BF16), wglu_ref[...]) + bglu_ref[...])
    z = jnp.concatenate(z_parts, axis=-1)
    out = _dot((y * _silu(z)).astype(BF16), wout_ref[...])
    o_ref[0] = _tail(x, out, pe, wpg_ref, lng_ref, lnb_ref, alpha)


def _s5_layer(x, p, layer, w_in, lam_re, lam_im, log_dt, b_re, b_im, c_re, c_im, d_skip, w_glu, b_glu, w_out,
              wpe, wpg, ln_g, ln_b, alpha):
    b, l, d = x.shape
    e = w_out.shape[0]
    g, pst = lam_re.shape
    hch = b_re.shape[-1]
    nslab = g // SLAB_GROUPS
    w = SLAB_GROUPS * pst
    tm = min(ROW_TILE, l)
    seg_len = tm // SUBLANES
    assert seg_len & (seg_len - 1) == 0, "segment length must be a power of two"

    gp1 = jax.ShapeDtypeStruct((g, pst), F32)
    gp = jax.ShapeDtypeStruct((SUBLANES, g, pst), F32)
    gb = jax.ShapeDtypeStruct((g, hch, pst), F32)
    a_re, a_im, pw_re, pw_im, bb_re, bb_im = pl.pallas_call(
        functools.partial(_s5_discretise_kernel, seg_len=seg_len),
        out_shape=(gp1, gp1, gp, gp, gb, gb), name="s5_discretise",
    )(lam_re, lam_im, log_dt.reshape(g, 1), jnp.swapaxes(b_re, 1, 2), jnp.swapaxes(b_im, 1, 2))

    eye = jnp.eye(SLAB_GROUPS, dtype=F32)
    blk_in = lambda m: jnp.einsum("sghp,gk->sghkp", m.reshape(nslab, SLAB_GROUPS, hch, pst),
                                  eye).reshape(nslab, LANES, w).astype(BF16)
    blk_out = lambda m: jnp.einsum("sghp,gk->skpgh", m.reshape(nslab, SLAB_GROUPS, hch, pst),
                                   eye).reshape(nslab, w, LANES).astype(BF16)
    rows = jnp.arange(SUBLANES)[:, None, None]
    flat = lambda m: m.reshape(SUBLANES, nslab, w)
    tabs = [jnp.broadcast_to(m.reshape(1, nslab, w), (SUBLANES, nslab, w)) for m in (a_re, a_im)]
    for dist in (1, 2, 4):
        tabs += [jnp.where(rows >= dist, flat(m)[dist - 1][None], 0.0) for m in (pw_re, pw_im)]
    tabs.append(jnp.concatenate([jnp.ones((1, nslab, w), F32), flat(pw_re)[:SUBLANES - 1]]))
    tabs.append(jnp.concatenate([jnp.zeros((1, nslab, w), F32), flat(pw_im)[:SUBLANES - 1]]))
    tab = jnp.transpose(jnp.stack(tabs), (2, 0, 1, 3))

    return pl.pallas_call(
        functools.partial(_s5_layer_kernel, alpha=alpha),
        grid=(b, l // tm),
        in_specs=[_rows(tm, d), _layer_rows(layer, tm, p.shape[-1]), _full(w_in.shape),
                  _full((nslab, LANES, w)), _full((nslab, LANES, w)),
                  _full((nslab, w, LANES)), _full((nslab, w, LANES)),
                  _full(tab.shape), _full((1, e)),
                  _full(w_glu.shape), _full((1, e)), _full(w_out.shape), _full(wpe.shape),
                  _full(wpg.shape), _full((1, d)), _full((1, d))],
        out_specs=_rows(tm, d),
        out_shape=jax.ShapeDtypeStruct((b, l, d), F32),
        scratch_shapes=[pltpu.VMEM((2 * nslab, w), F32), pltpu.VMEM((nslab, tm, w), F32),
                        pltpu.VMEM((nslab, tm, w), F32), pltpu.VMEM((tm, e), F32)],
        compiler_params=_seq_params(),
        name="s5_layer",
    )(x, p, w_in.astype(BF16), blk_in(bb_re), blk_in(bb_im),
      blk_out(c_re), blk_out(-c_im),
      tab, d_skip.reshape(1, e), w_glu.astype(BF16), b_glu.reshape(1, e), w_out.astype(BF16),
      wpe.astype(BF16), wpg.astype(BF16), ln_g.reshape(1, d), ln_b.reshape(1, d))


def _rwkv_layer_kernel(x_ref, p_ref, mu_ref, wrkvz_ref, w0_ref, w1_ref, w2_ref, a0_ref, a1_ref, a2_ref,
                       kk_ref, ka_ref, rk_ref, lgx_ref, lbx_ref, wout_ref, wpe_ref, wpg_ref, lng_ref, lnb_ref,
                       o_ref, s_ref, carry_ref, *, alpha):
    nb, tc, d = x_ref.shape
    e = wout_ref.shape[0]
    n = s_ref.shape[1]
    nh = s_ref.shape[0] // nb
    rows = nb * tc

    @pl.when(pl.program_id(1) == 0)
    def _():
        s_ref[...] = jnp.zeros_like(s_ref)
        carry_ref[...] = jnp.zeros_like(carry_ref)

    proj = {}

    def project():
        x3 = x_ref[...]
        x = x3.reshape(rows, d)
        shifted = jnp.concatenate([_shift_rows(x3[b], carry_ref[b], 1) for b in range(nb)], axis=0)
        carry_ref[...] = x3[:, tc - SUBLANES:, :]
        dx = shifted - x
        mix = lambda i: (x + dx * mu_ref[i:i + 1, :]).astype(BF16)
        lora_w = _dot(jnp.tanh(_dot(mix(4), w1_ref[...])).astype(BF16), w2_ref[...])
        w_log = -jax.nn.softplus(-(w0_ref[...] + lora_w)) - 0.5
        proj.update(x=x, lw=-jnp.exp(w_log))
        yield
        proj["k"] = _dot(mix(1), wrkvz_ref[1])
        yield
        lora_a = _dot(_dot(mix(5), a1_ref[...]).astype(BF16), a2_ref[...])
        proj["a"] = jax.nn.sigmoid(a0_ref[...] + lora_a)
        yield
        proj["r"] = _dot(mix(0), wrkvz_ref[0])
        yield
        proj["v"] = _dot(mix(2), wrkvz_ref[2])
        yield
        proj["z"] = _dot(mix(3), wrkvz_ref[3])

    seg_w = 2 * LANES
    srow = lax.broadcasted_iota(jnp.int32, (seg_w, seg_w), 0)
    scol = lax.broadcasted_iota(jnp.int32, (seg_w, seg_w), 1)
    seg_ones = jnp.where(srow // n == scol // n, 1.0, 0.0).astype(BF16)

    def seg_sum(t, terms):
        pieces = _split(t, terms)
        return jnp.concatenate([sum(_dot(piece[:, j:j + seg_w], seg_ones) for piece in pieces)
                                for j in range(0, e, seg_w)], axis=-1)

    ngroups = RWKV_GROUPS if nb % RWKV_GROUPS == 0 else 1
    gb = nb // ngroups
    grows = gb * tc
    heads = lambda t: jnp.stack([t[(t.shape[0] // gb) * b:(t.shape[0] // gb) * (b + 1), n * h:n * (h + 1)]
                                 for b in range(gb) for h in range(nh)])
    hb = lambda t: heads(t.astype(BF16))
    bdot = lambda spec, lhs, rhs: jnp.einsum(spec, lhs, rhs, preferred_element_type=F32)
    brow = lax.broadcasted_iota(jnp.int32, (grows, grows), 0)
    bcol = lax.broadcasted_iota(jnp.int32, (grows, grows), 1)
    tri = jnp.where((brow >= bcol) & (brow // tc == bcol // tc), 1.0, 0.0).astype(BF16)
    row2 = lax.broadcasted_iota(jnp.int32, (tc, 2 * tc), 0)
    lane2 = lax.broadcasted_iota(jnp.int32, (tc, 2 * tc), 1)
    right = (lane2 >= tc)[None]
    col2 = jnp.where(lane2 >= tc, lane2 - tc, lane2)

    def prepare(gi, ops):
        sl = slice(gi * grows, (gi + 1) * grows)
        kg, ag, lwg = proj["k"][sl], proj["a"][sl], proj["lw"][sl]
        g = sum(_dot(tri, piece) for piece in _split(lwg, 2))
        g_end = g.reshape(gb, tc, e)[:, tc - 1:, :]
        yield
        kmod = kg * (1.0 + (ag - 1.0) * ka_ref[...])
        e_neg = jnp.exp(-g)
        yield
        e_end = jnp.exp(g_end - g.reshape(gb, tc, e)).reshape(grows, e)
        kkr = kg * kk_ref[...]
        kk = kkr / jnp.maximum(jnp.sqrt(seg_sum(kkr * kkr, 2)), 1e-12)
        bvec = kk * ag
        yield
        ops["at"] = hb(-kk * jnp.exp(g - lwg))
        yield
        rg = proj["r"][sl]
        ops["rt"] = hb(rg * jnp.exp(g))
        yield
        ops["btkt"] = jnp.concatenate([hb(bvec * e_neg), hb(kmod * e_neg)], axis=1)
        yield
        ops["bhkh"] = jnp.concatenate([hb(bvec * e_end), hb(kmod * e_end)], axis=1)
        yield
        vg = proj["v"][sl]
        ops.update(v=vg, v_h=hb(vg), decay=heads(jnp.exp(g_end).reshape(gb, e)),
                   bonus=rg * kmod * rk_ref[...])

    def solve(gi, ops):
        v_h = ops["v_h"]
        hs = slice(gi * gb * nh, (gi + 1) * gb * nh)
        s0 = s_ref[hs]
        both = bdot("htk,hsk->hts", jnp.concatenate([ops["at"], ops["rt"]], axis=1),
                    jnp.concatenate([ops["btkt"], s0.astype(BF16)], axis=1))
        aa, from_state = both[:, :, :2 * tc], both[:, :, 2 * tc:]
        yield
        m_a = jnp.where((row2 > col2)[None], aa[:, :tc, :], 0.0).astype(BF16)
        m_r = jnp.where((row2 >= col2)[None], aa[:, tc:, :], 0.0).astype(BF16)
        rhs = from_state[:, :tc] + bdot(
            "hts,hsv->htv", m_a, jnp.concatenate([jnp.zeros_like(v_h), v_h], axis=1))
        yield
        pair = jnp.where(right, jnp.where(row2 == col2, 1.0, 0.0)[None], m_a.astype(F32))
        span = 1
        while span < tc:
            pair_b = pair.astype(BF16)
            pair = bdot("hts,hsu->htu", pair_b[:, :, :tc], pair_b) + jnp.where(right, pair, 0.0)
            span *= 2
            yield
        rhs_b = rhs.astype(BF16)
        sa = bdot("hts,hsv->htv", pair.astype(BF16), jnp.concatenate([jnp.zeros_like(rhs_b), rhs_b], axis=1))
        yield
        sv = jnp.concatenate([sa.astype(BF16), v_h], axis=1)
        out = from_state[:, tc:] + bdot("hts,hsv->htv", m_r, sv)
        s_ref[hs] = s0 * ops["decay"] + bdot("htv,htk->hvk", sv, ops["bhkh"])
        yield
        out = jnp.concatenate(
            [jnp.concatenate([out[b * nh + h] for h in range(nh)], axis=-1) for b in range(gb)], axis=0)
        dev = out - seg_sum(out, 1) * (1.0 / n)
        var = seg_sum(dev * dev, 1) * (1.0 / n)
        ops["res"] = (dev * lax.rsqrt(var + RWKV_GN_EPS) * lgx_ref[...] + lbx_ref[...]
                      + seg_sum(ops["bonus"], 1) * ops["v"])

    def finish(gi, ops):
        sl, bs = slice(gi * grows, (gi + 1) * grows), slice(gi * gb, (gi + 1) * gb)
        y = _dot((ops["res"] * _silu(proj["z"][sl])).astype(BF16), wout_ref[...])
        yield
        pe = _dot(p_ref[0, bs].reshape(grows, p_ref.shape[-1]).astype(BF16), wpe_ref[...])
        o_ref[bs] = _tail(proj["x"][sl], y, pe, wpg_ref, lng_ref, lnb_ref, alpha).reshape(gb, tc, d)

    def alternate(*stages):
        stages = list(stages)
        while stages:
            for gen in list(stages):
                if next(gen, stages) is stages:
                    stages.remove(gen)

    ops = [dict() for _ in range(ngroups)]
    projecting = project()
    for _ in range(3):
        next(projecting)
    alternate(projecting, prepare(0, ops[0]))
    for gi in range(ngroups):
        alternate(solve(gi, ops[gi]),
                  *([prepare(gi + 1, ops[gi + 1])] if gi + 1 < ngroups else []),
                  *([finish(gi - 1, ops[gi - 1])] if gi >= 1 else []))
    alternate(finish(ngroups - 1, ops[ngroups - 1]))


def _rwkv_layer(x, p, layer, mu, w_rkvz, w0, w1, w2, a0, a1, a2, k_k, k_a, r_k, lnx_g, lnx_b, w_out,
                wpe, wpg, ln_g, ln_b, alpha):
    b, l, d = x.shape
    e = w_out.shape[0]
    nh = e // RWKV_HEAD
    tc = min(RWKV_CHUNK, l)
    nb = RWKV_BATCH if b % RWKV_BATCH == 0 else 1
    row1 = lambda t: t.reshape(1, -1)
    chunk_rows = lambda width: pl.BlockSpec((nb, tc, width), lambda i, t: (i, t, 0))
    return pl.pallas_call(
        functools.partial(_rwkv_layer_kernel, alpha=alpha),
        grid=(b // nb, l // tc),
        in_specs=[chunk_rows(d), pl.BlockSpec((1, nb, tc, p.shape[-1]), lambda i, t: (layer, i, t, 0)),
                  _full(mu.shape), _full(w_rkvz.shape), _full((1, e)), _full(w1.shape), _full(w2.shape),
                  _full((1, e)), _full(a1.shape), _full(a2.shape)] + [_full((1, e))] * 5
                 + [_full(w_out.shape), _full(wpe.shape), _full(wpg.shape), _full((1, d)), _full((1, d))],
        out_specs=chunk_rows(d),
        out_shape=jax.ShapeDtypeStruct((b, l, d), F32),
        scratch_shapes=[pltpu.VMEM((nb * nh, RWKV_HEAD, RWKV_HEAD), F32),
                        pltpu.VMEM((nb, SUBLANES, d), F32)],
        compiler_params=_seq_params(),
        name="rwkv_layer",
    )(x, p, mu, w_rkvz.astype(BF16), row1(w0), w1.astype(BF16), w2.astype(BF16), row1(a0),
      a1.astype(BF16), a2.astype(BF16), row1(k_k), row1(k_a), row1(r_k), row1(lnx_g), row1(lnx_b),
      w_out.astype(BF16), wpe.astype(BF16), wpg.astype(BF16), row1(ln_g), row1(ln_b))


def kernel(x, p, conv_w_in, conv_k, conv_w_out, ssm_w_in, ssm_lam_re, ssm_lam_im, ssm_log_dt, ssm_b_re, ssm_b_im, ssm_c_re, ssm_c_im, ssm_d, ssm_w_glu, ssm_b_glu, ssm_w_out, rwkv_mu, rwkv_w_rkvz, rwkv_w0, rwkv_w1, rwkv_w2, rwkv_a0, rwkv_a1, rwkv_a2, rwkv_k_k, rwkv_k_a, rwkv_r_k, rwkv_lnx_g, rwkv_lnx_b, rwkv_w_out, ple_proj, ple_gate, ln_g, ln_b):
    depth = p.shape[0]
    alpha = (2 * depth) ** 0.25
    for i in range(depth):
        kind, j = i % 3, i // 3
        post = (ple_proj[i], ple_gate[i], ln_g[i], ln_b[i], alpha)
        if kind == 0:
            x = _conv_layer(x, p, i, conv_w_in[j], conv_k[j], conv_w_out[j], *post)
        elif kind == 1:
            x = _s5_layer(x, p, i, ssm_w_in[j], ssm_lam_re[j], ssm_lam_im[j], ssm_log_dt[j],
                          ssm_b_re[j], ssm_b_im[j], ssm_c_re[j], ssm_c_im[j], ssm_d[j],
                          ssm_w_glu[j], ssm_b_glu[j], ssm_w_out[j], *post)
        else:
            x = _rwkv_layer(x, p, i, rwkv_mu[j], rwkv_w_rkvz[j], rwkv_w0[j], rwkv_w1[j], rwkv_w2[j],
                            rwkv_a0[j], rwkv_a1[j], rwkv_a2[j], rwkv_k_k[j], rwkv_k_a[j],
                            rwkv_r_k[j].reshape(-1), rwkv_lnx_g[j], rwkv_lnx_b[j], rwkv_w_out[j], *post)
    return x
```
